```python
import jax
import jax.numpy as jnp
from jax import lax
import numpy as np

D_MODEL = 4096
BATCH = 1
SEQ = 16384
DEPTH = 2

N_A_LAYERS = DEPTH // 2
N_B_LAYERS = DEPTH - N_A_LAYERS
D_FF = 4 * D_MODEL
EPS = 1e-6
N_MOD = 6

GLA_HEADS = D_MODEL // 512
GLA_KEY_WIDTH = D_MODEL // 2
GLA_DK = GLA_KEY_WIDTH // GLA_HEADS
GLA_DV = D_MODEL // GLA_HEADS
GLA_GATE_RANK = 16
GLA_TAU = 16.0
GLA_CHUNK = 64
GLA_IN_WIDTH = 2 * GLA_KEY_WIDTH + 2 * D_MODEL + GLA_GATE_RANK

FOX_HEAD_DIM = 128
FOX_HEADS = D_MODEL // FOX_HEAD_DIM
FOX_BLOCK = 128
FOX_KV_WIDTH = 2 * D_MODEL + FOX_HEADS

kernel_name = "yoco_gla_fox_hybrid"


def rms_norm(x, gain):
    x32 = x.astype(jnp.float32)
    y = x32 * lax.rsqrt(jnp.mean(x32 * x32, axis=-1, keepdims=True) + EPS)
    return y.astype(x.dtype) * gain


def modulate(h, shift, scale):
    return h * (1 + scale[:, None, :]) + shift[:, None, :]


def ada_params(c, w, b, n):
    return jnp.split(jax.nn.silu(c) @ w + b, n, axis=-1)


def sqrelu_mlp(h, w_up, w_down):
    return jnp.square(jax.nn.relu(h @ w_up)) @ w_down


def gla_mixer(h, w_in, w_alpha_up, b_alpha, o_norm, w_out):
    B, S, _ = h.shape
    nc = S // GLA_CHUNK
    q, k, v, r, a_lo = jnp.split(
        h @ w_in,
        [GLA_KEY_WIDTH, 2 * GLA_KEY_WIDTH, 2 * GLA_KEY_WIDTH + D_MODEL, 2 * GLA_KEY_WIDTH + 2 * D_MODEL],
        axis=-1)
    log_alpha = jax.nn.log_sigmoid((a_lo @ w_alpha_up + b_alpha).astype(jnp.float32)) / GLA_TAU

    def to_chunks(t, dh):
        return t.astype(jnp.float32).reshape(B, nc, GLA_CHUNK, GLA_HEADS, dh).transpose(1, 0, 3, 2, 4)

    qc = to_chunks(q * GLA_DK ** -0.5, GLA_DK)
    kc = to_chunks(k, GLA_DK)
    vc = to_chunks(v, GLA_DV)
    lac = to_chunks(log_alpha, GLA_DK)
    causal = jnp.tril(jnp.ones((GLA_CHUNK, GLA_CHUNK), dtype=bool))

    def step(state, inp):
        q_c, k_c, v_c, la_c = inp
        b = jnp.cumsum(la_c, axis=2)
        o_inter = jnp.einsum('bhik,bhkv->bhiv', q_c * jnp.exp(b), state)
        rel = jnp.where(causal[:, :, None], b[:, :, :, None, :] - b[:, :, None, :, :], -jnp.inf)
        scores = jnp.einsum('bhik,bhjk,bhijk->bhij', q_c, k_c, jnp.exp(rel))
        o_intra = jnp.einsum('bhij,bhjv->bhiv', scores, v_c)
        b_last = b[:, :, -1, :]
        new_state = jnp.exp(b_last)[..., None] * state + jnp.einsum(
            'bhjk,bhjv->bhkv', k_c * jnp.exp(b_last[:, :, None, :] - b), v_c)
        return new_state, o_inter + o_intra

    state0 = jnp.zeros((B, GLA_HEADS, GLA_DK, GLA_DV), jnp.float32)
    _, o = lax.scan(step, state0, (qc, kc, vc, lac))
    o = o.transpose(1, 0, 3, 2, 4).reshape(B, S, GLA_HEADS, GLA_DV).astype(h.dtype)
    o = rms_norm(o, o_norm).reshape(B, S, D_MODEL) * jax.nn.silu(r)
    return o @ w_out


def shared_kv(x, c, kv_ada_w, kv_ada_b, kv_norm, kv_w, fox_b_f, fox_k_norm):
    B, S, _ = x.shape
    shift, scale = ada_params(c, kv_ada_w, kv_ada_b, 2)
    h = modulate(rms_norm(x, kv_norm), shift, scale)
    k, v, f_logit = jnp.split(h @ kv_w, [D_MODEL, 2 * D_MODEL], axis=-1)
    k = rms_norm(k.reshape(B, S, FOX_HEADS, FOX_HEAD_DIM), fox_k_norm)
    v = v.reshape(B, S, FOX_HEADS, FOX_HEAD_DIM)
    log_f = jax.nn.log_sigmoid(f_logit.astype(jnp.float32) + fox_b_f)
    cum_log_f = jnp.cumsum(log_f, axis=1).transpose(0, 2, 1)
    return k, v, cum_log_f


def fox_mixer(h, w_in, q_norm, w_out, k, v, cum_log_f):
    B, S, _ = h.shape
    nb = S // FOX_BLOCK
    q, og = jnp.split(h @ w_in, 2, axis=-1)
    q = rms_norm(q.reshape(B, S, FOX_HEADS, FOX_HEAD_DIM), q_norm) * FOX_HEAD_DIM ** -0.5
    q_blocks = q.reshape(B, nb, FOX_BLOCK, FOX_HEADS, FOX_HEAD_DIM).transpose(1, 0, 2, 3, 4)
    cq_blocks = cum_log_f.reshape(B, FOX_HEADS, nb, FOX_BLOCK).transpose(2, 0, 1, 3)
    pos = jnp.arange(S, dtype=jnp.int32)
    pos_blocks = pos.reshape(nb, FOX_BLOCK)

    def attend(args):
        q_b, cq_b, qpos = args
        logits = jnp.einsum('bqhd,bkhd->bhqk', q_b, k).astype(jnp.float32)
        logits = logits + cq_b[..., :, None] - cum_log_f[:, :, None, :]
        logits = jnp.where(qpos[:, None] >= pos[None, :], logits, -jnp.inf)
        p = jax.nn.softmax(logits, axis=-1)
        return jnp.einsum('bhqk,bkhd->bqhd', p.astype(v.dtype), v)

    o = lax.map(attend, (q_blocks, cq_blocks, pos_blocks))
    o = o.transpose(1, 0, 2, 3, 4).reshape(B, S, D_MODEL)
    return (o * jax.nn.sigmoid(og)) @ w_out


def setup_inputs(seed: int = 0) -> dict:
    key = jax.random.key(seed)
    ks = jax.random.split(key, 24)
    D = D_MODEL
    f32 = jnp.float32

    def nrm(k, shape, fan_in):
        return jax.random.normal(k, shape, f32) * fan_in ** -0.5

    def gain(k, shape):
        return 1.0 + 0.02 * jax.random.normal(k, shape, f32)

    def bias(k, shape):
        return 0.02 * jax.random.normal(k, shape, f32)

    return {
        "x": jax.random.normal(ks[0], (BATCH, SEQ, D), f32),
        "c": jax.random.normal(ks[1], (BATCH, D), f32),
        "ada_w": nrm(ks[2], (DEPTH, D, N_MOD * D), D),
        "ada_b": bias(ks[3], (DEPTH, N_MOD * D)),
        "norm_mix": gain(ks[4], (DEPTH, D)),
        "norm_mlp": gain(ks[5], (DEPTH, D)),
        "w_mlp_up": nrm(ks[6], (DEPTH, D, D_FF), D),
        "w_mlp_down": nrm(ks[7], (DEPTH, D_FF, D), D_FF),
        "gla_w_in": nrm(ks[8], (N_A_LAYERS, D, GLA_IN_WIDTH), D),
        "gla_w_alpha_up": nrm(ks[9], (N_A_LAYERS, GLA_GATE_RANK, GLA_KEY_WIDTH), GLA_GATE_RANK),
        "gla_b_alpha": bias(ks[10], (N_A_LAYERS, GLA_KEY_WIDTH)),
        "gla_o_norm": gain(ks[11], (N_A_LAYERS, GLA_DV)),
        "gla_w_out": nrm(ks[12], (N_A_LAYERS, D, D), D),
        "kv_ada_w": nrm(ks[13], (D, 2 * D), D),
        "kv_ada_b": bias(ks[14], (2 * D,)),
        "kv_norm": gain(ks[15], (D,)),
        "kv_w": nrm(ks[16], (D, FOX_KV_WIDTH), D),
        "fox_b_f": jax.random.uniform(ks[17], (FOX_HEADS,), f32, minval=1.0, maxval=6.0),
        "fox_k_norm": gain(ks[18], (FOX_HEAD_DIM,)),
        "fox_w_in": nrm(ks[19], (N_B_LAYERS, D, 2 * D), D),
        "fox_q_norm": gain(ks[20], (N_B_LAYERS, FOX_HEAD_DIM)),
        "fox_w_out": nrm(ks[21], (N_B_LAYERS, D, D), D),
    }


def reference(x, c, ada_w, ada_b, norm_mix, norm_mlp, w_mlp_up, w_mlp_down,
              gla_w_in, gla_w_alpha_up, gla_b_alpha, gla_o_norm, gla_w_out,
              kv_ada_w, kv_ada_b, kv_norm, kv_w, fox_b_f, fox_k_norm,
              fox_w_in, fox_q_norm, fox_w_out):
    shared = None
    for l in range(DEPTH):
        sh1, sc1, g1, sh2, sc2, g2 = ada_params(c, ada_w[l], ada_b[l], N_MOD)
        h = modulate(rms_norm(x, norm_mix[l]), sh1, sc1)
        if l < N_A_LAYERS:
            y = gla_mixer(h, gla_w_in[l], gla_w_alpha_up[l], gla_b_alpha[l], gla_o_norm[l], gla_w_out[l])
        else:
            if shared is None:
                shared = shared_kv(x, c, kv_ada_w, kv_ada_b, kv_norm, kv_w, fox_b_f, fox_k_norm)
            j = l - N_A_LAYERS
            y = fox_mixer(h, fox_w_in[j], fox_q_norm[j], fox_w_out[j], shared[0], shared[1], shared[2])
        x = x + g1[:, None, :] * y
        h = modulate(rms_norm(x, norm_mlp[l]), sh2, sc2)
        x = x + g2[:, None, :] * sqrelu_mlp(h, w_mlp_up[l], w_mlp_down[l])
    return x
```

```python
import functools

import numpy as np
import jax
import jax.numpy as jnp
from jax import lax
from jax.experimental import pallas as pl
from jax.experimental.pallas import tpu as pltpu

EPS = 1e-6
LANES = 128
GLA_DK = 256
GLA_DV = 512
GLA_TAU = 16.0
GLA_CHUNK = 128
FOX_HEAD = 128
NEG_BIG = -1e30
VMEM_LIMIT_BYTES = 56 * 1024 * 1024

F32 = jnp.float32
BF16 = jnp.bfloat16


def _params(*sem):
    return pltpu.CompilerParams(dimension_semantics=sem, vmem_limit_bytes=VMEM_LIMIT_BYTES)


def _log_sigmoid(x):
    return jnp.minimum(x, 0.0) - jnp.log1p(jnp.exp(-jnp.abs(x)))


def _silu(x):
    return x * jax.nn.sigmoid(x)


def _ada_body(c_ref, w_ref, b_ref, o_ref, cb_ref, *, tn):
    @pl.when(pl.program_id(0) == 0)
    def _():
        cc = c_ref[...]
        cb_ref[...] = jnp.broadcast_to(_silu(cc), cb_ref.shape)
    cb = cb_ref[...]
    parts = []
    for a in range(tn // LANES):
        parts.append(jnp.sum(w_ref[:, a * LANES:(a + 1) * LANES] * cb, axis=0, keepdims=True))
    o_ref[...] = jnp.concatenate(parts, axis=1) + b_ref[...]


def _ada(c_col, w, b_row, layer):
    D = c_col.shape[0]
    N = w.shape[-1]
    tn = min(512, N)
    if w.ndim == 3:
        w_spec = pl.BlockSpec((None, D, tn), lambda j: (layer, 0, j))
    else:
        w_spec = pl.BlockSpec((D, tn), lambda j: (0, j))
    return pl.pallas_call(
        functools.partial(_ada_body, tn=tn),
        grid=(N // tn,),
        in_specs=[pl.BlockSpec((D, 1), lambda j: (0, 0)), w_spec,
                  pl.BlockSpec((1, tn), lambda j: (0, j))],
        out_specs=pl.BlockSpec((1, tn), lambda j: (0, j)),
        out_shape=jax.ShapeDtypeStruct((1, N), F32),
        scratch_shapes=[pltpu.VMEM((D, LANES), F32)],
        compiler_params=_params("arbitrary"),
        name="ada_gemv",
    )(c_col, w, b_row)


def _normmod_body(x_ref, g_ref, sh_ref, sc_ref, o_ref):
    x = x_ref[...]
    ms = jnp.mean(x * x, axis=-1, keepdims=True)
    y = x * lax.rsqrt(ms + EPS) * g_ref[...]
    o_ref[...] = (y * (1.0 + sc_ref[...]) + sh_ref[...]).astype(o_ref.dtype)


def _norm_mod(x, gain, shift, scale):
    S, D = x.shape
    tr = min(256, S)
    row = pl.BlockSpec((1, D), lambda i: (0, 0))
    return pl.pallas_call(
        _normmod_body,
        grid=(S // tr,),
        in_specs=[pl.BlockSpec((tr, D), lambda i: (i, 0)), row, row, row],
        out_specs=pl.BlockSpec((tr, D), lambda i: (i, 0)),
        out_shape=jax.ShapeDtypeStruct((S, D), BF16),
        compiler_params=_params("parallel"),
        name="norm_mod",
    )(x, gain, shift, scale)


def _ep_cast(acc, extra, o_ref):
    o_ref[...] = acc.astype(o_ref.dtype)


def _ep_sqrelu(acc, extra, o_ref):
    r = jnp.maximum(acc, 0.0)
    o_ref[...] = (r * r).astype(o_ref.dtype)


def _ep_resid(acc, extra, o_ref):
    x_ref, g_ref = extra
    o_ref[...] = x_ref[...] + g_ref[...] * acc


def _ep_headnorm(acc, extra, o_ref, *, n_norm_blocks, scale):
    (gain_ref,) = extra
    j = pl.program_id(1)

    @pl.when(j < n_norm_blocks)
    def _():
        gain = gain_ref[...]
        for g in range(acc.shape[1] // FOX_HEAD):
            blk = acc[:, g * FOX_HEAD:(g + 1) * FOX_HEAD]
            ms = jnp.mean(blk * blk, axis=-1, keepdims=True)
            y = blk * lax.rsqrt(ms + EPS) * gain * scale
            o_ref[:, g * FOX_HEAD:(g + 1) * FOX_HEAD] = y.astype(o_ref.dtype)

    @pl.when(j >= n_norm_blocks)
    def _():
        o_ref[...] = acc.astype(o_ref.dtype)


def _mm_body(*refs, nk, n_extra, epilogue):
    a_ref, b_ref = refs[0], refs[1]
    extra = refs[2:2 + n_extra]
    o_ref = refs[2 + n_extra]
    if nk == 1:
        epilogue(jnp.dot(a_ref[...], b_ref[...], preferred_element_type=F32), extra, o_ref)
        return
    acc_ref = refs[3 + n_extra]
    k = pl.program_id(2)

    @pl.when(k == 0)
    def _():
        acc_ref[...] = jnp.zeros_like(acc_ref)

    acc_ref[...] += jnp.dot(a_ref[...], b_ref[...], preferred_element_type=F32)

    @pl.when(k == nk - 1)
    def _():
        epilogue(acc_ref[...], extra, o_ref)


def _matmul(a, b, *, out_dtype, epilogue=_ep_cast, extras=(), extra_specs=(), name):
    M, K = a.shape
    N = b.shape[1]
    tm, tn, tk = min(1024, M), min(1024, N), min(1024, K)
    nk = K // tk
    scratch = [pltpu.VMEM((tm, tn), F32)] if nk > 1 else []
    return pl.pallas_call(
        functools.partial(_mm_body, nk=nk, n_extra=len(extras), epilogue=epilogue),
        grid=(M // tm, N // tn, nk),
        in_specs=[pl.BlockSpec((tm, tk), lambda i, j, k: (i, k)),
                  pl.BlockSpec((tk, tn), lambda i, j, k: (k, j)), *extra_specs],
        out_specs=pl.BlockSpec((tm, tn), lambda i, j, k: (i, j)),
        out_shape=jax.ShapeDtypeStruct((M, N), out_dtype),
        scratch_shapes=scratch,
        compiler_params=_params("parallel", "parallel", "arbitrary"),
        name=name,
    )(a, b, *extras)


def _matmul_resid(a, b, x, gate, name):
    M, N = x.shape
    tm, tn = min(1024, M), min(1024, N)
    return _matmul(a, b, out_dtype=F32, epilogue=_ep_resid, extras=(x, gate),
                   extra_specs=(pl.BlockSpec((tm, tn), lambda i, j, k: (i, j)),
                                pl.BlockSpec((1, tn), lambda i, j, k: (0, j))), name=name)


def _matmul_headnorm(a, b, gain, n_norm_cols, scale, name):
    tn = min(1024, b.shape[1])
    ep = functools.partial(_ep_headnorm, n_norm_blocks=n_norm_cols // tn, scale=scale)
    return _matmul(a, b, out_dtype=BF16, epilogue=ep, extras=(gain,),
                   extra_specs=(pl.BlockSpec((1, FOX_HEAD), lambda i, j, k: (0, 0)),), name=name)


def _gla_prefix_matrix(C):
    L = int(np.log2(C))
    t = np.arange(C)[:, None]
    j = np.arange(C)[None, :]
    mats = []
    for l in range(L):
        n = 1 << l
        mid = ((t >> (l + 1)) << (l + 1)) + n
        upper = ((t >> l) & 1) == 1
        mats.append(np.where(upper, (j >= mid) & (j <= t), (j > t) & (j < mid)))
    mats.append(j <= t)
    mats.append(j > t)
    return np.concatenate(mats, axis=0).astype(np.float32)


def _gla_body(q_ref, k_ref, v_ref, r_ref, alo_ref, wup_ref, balpha_ref, onorm_ref, f_ref,
              o_ref, state_ref, *, C, L):
    c = pl.program_id(0)
    h = pl.program_id(1)

    @pl.when(c == 0)
    def _():
        state_ref[h] = jnp.zeros(state_ref.shape[1:], F32)

    q = q_ref[...].astype(F32) * (GLA_DK ** -0.5)
    k = k_ref[...].astype(F32)
    v = v_ref[...]

    z = jnp.dot(alo_ref[...].astype(BF16), wup_ref[...], preferred_element_type=F32) + balpha_ref[...]
    la = _log_sigmoid(z) * (1.0 / GLA_TAU)
    hi = la.astype(BF16)
    mid = (la - hi.astype(F32)).astype(BF16)
    xs = jnp.dot(f_ref[...], jnp.concatenate([hi, mid], axis=1), preferred_element_type=F32)
    xs = xs[:, :GLA_DK] + xs[:, GLA_DK:]

    row = lax.broadcasted_iota(jnp.int32, (C, C), 0)
    col = lax.broadcasted_iota(jnp.int32, (C, C), 1)
    scores = jnp.where(row == col, jnp.sum(q * k, axis=-1, keepdims=True), 0.0)
    for l in range(L):
        e = jnp.exp(xs[l * C:(l + 1) * C])
        s_l = lax.dot_general((q * e).astype(BF16), (k * e).astype(BF16),
                              (((1,), (1,)), ((), ())), preferred_element_type=F32)
        rl = row >> l
        cl = col >> l
        scores = jnp.where(((rl ^ cl) == 1) & (rl > cl), s_l, scores)

    b = xs[L * C:(L + 1) * C]
    rem = xs[(L + 1) * C:]

    st = state_ref[h]
    o = jnp.dot(scores.astype(BF16), v, preferred_element_type=F32)
    o += jnp.dot((q * jnp.exp(b)).astype(BF16), st.astype(BF16), preferred_element_type=F32)

    kt = jnp.transpose(k * jnp.exp(rem)).astype(BF16)
    d_row = jnp.exp(b[C - 1:C, :])
    d_col = jnp.transpose(jnp.broadcast_to(d_row, (LANES, GLA_DK)))[:, :1]
    state_ref[h] = d_col * st + jnp.dot(kt, v, preferred_element_type=F32)

    ms = jnp.mean(o * o, axis=-1, keepdims=True)
    y = o * lax.rsqrt(ms + EPS) * onorm_ref[...]
    o_ref[...] = (y * _silu(r_ref[...].astype(F32))).astype(o_ref.dtype)


def _gla(proj, a_lo, w_up, b_alpha, o_norm, D):
    S = proj.shape[0]
    H = D // GLA_DV
    KW = H * GLA_DK
    C = min(GLA_CHUNK, S)
    L = int(np.log2(C))
    fmat = jnp.asarray(_gla_prefix_matrix(C), BF16)
    kq, kv, kr = KW // GLA_DK, (2 * KW) // GLA_DV, (2 * KW + D) // GLA_DV
    return pl.pallas_call(
        functools.partial(_gla_body, C=C, L=L),
        grid=(S // C, H),
        in_specs=[
            pl.BlockSpec((C, GLA_DK), lambda c, h: (c, h)),
            pl.BlockSpec((C, GLA_DK), lambda c, h: (c, kq + h)),
            pl.BlockSpec((C, GLA_DV), lambda c, h: (c, kv + h)),
            pl.BlockSpec((C, GLA_DV), lambda c, h: (c, kr + h)),
            pl.BlockSpec((C, LANES), lambda c, h: (c, 0)),
            pl.BlockSpec((LANES, GLA_DK), lambda c, h: (0, h)),
            pl.BlockSpec((1, GLA_DK), lambda c, h: (0, h)),
            pl.BlockSpec((1, GLA_DV), lambda c, h: (0, 0)),
            pl.BlockSpec(fmat.shape, lambda c, h: (0, 0)),
        ],
        out_specs=pl.BlockSpec((C, GLA_DV), lambda c, h: (c, h)),
        out_shape=jax.ShapeDtypeStruct((S, D), BF16),
        scratch_shapes=[pltpu.VMEM((H, GLA_DK, GLA_DV), F32)],
        compiler_params=_params("arbitrary", "arbitrary"),
        name="gla_chunk",
    )(proj, proj, proj, proj, a_lo, w_up, b_alpha, o_norm, fmat)


def _fcum_body(f_ref, bf_ref, tri_ref, o_ref, carry_ref, *, T):
    @pl.when(pl.program_id(0) == 0)
    def _():
        carry_ref[...] = jnp.zeros_like(carry_ref)

    lf = _log_sigmoid(f_ref[...] + bf_ref[...])
    hi = lf.astype(BF16)
    r1 = lf - hi.astype(F32)
    mid = r1.astype(BF16)
    lo = (r1 - mid.astype(F32)).astype(BF16)
    cs = jnp.dot(tri_ref[...], jnp.concatenate([hi, mid, lo], axis=1), preferred_element_type=F32)
    cum = cs[:, :LANES] + cs[:, LANES:2 * LANES] + cs[:, 2 * LANES:] + carry_ref[...]
    carry_ref[...] = cum[T - 1:T, :]
    o_ref[...] = jnp.transpose(cum)


def _forget_cumsum(f_logit, b_f):
    S = f_logit.shape[0]
    T = min(512, S)
    tri = jnp.asarray(np.tril(np.ones((T, T), np.float32)), BF16)
    return pl.pallas_call(
        functools.partial(_fcum_body, T=T),
        grid=(S // T,),
        in_specs=[pl.BlockSpec((T, LANES), lambda i: (i, 0)),
                  pl.BlockSpec((1, LANES), lambda i: (0, 0)),
                  pl.BlockSpec((T, T), lambda i: (0, 0))],
        out_specs=pl.BlockSpec((LANES, T), lambda i: (0, i)),
        out_shape=jax.ShapeDtypeStruct((LANES, S), F32),
        scratch_shapes=[pltpu.VMEM((1, LANES), F32)],
        compiler_params=_params("arbitrary"),
        name="forget_cumsum",
    )(f_logit, b_f, tri)


def _fox_body(q_ref, k_ref, v_ref, ck_ref, og_ref, o_ref, *, tq, tk):
    i = pl.program_id(1)
    q = q_ref[...]
    r = tq // tk

    def step(cidx, carry, diag_offset):
        m, l, acc = carry
        c0 = pl.multiple_of(cidx * tk, tk)
        kc = k_ref[pl.ds(c0, tk), :]
        vc = v_ref[pl.ds(c0, tk), :]
        s = lax.dot_general(q, kc, (((1,), (1,)), ((), ())), preferred_element_type=F32)
        s = s - ck_ref[pl.ds(cidx, 1), :]
        if diag_offset is not None:
            row = lax.broadcasted_iota(jnp.int32, (tq, tk), 0)
            col = lax.broadcasted_iota(jnp.int32, (tq, tk), 1)
            s = jnp.where(row >= col + diag_offset, s, NEG_BIG)
        m_new = jnp.maximum(m, jnp.max(s, axis=-1, keepdims=True))
        alpha = jnp.exp(m - m_new)
        p = jnp.exp(s - m_new)
        l = alpha * l + jnp.sum(p, axis=-1, keepdims=True)
        acc = alpha * acc + jnp.dot(p.astype(BF16), vc, preferred_element_type=F32)
        return m_new, l, acc

    carry = (jnp.full((tq, 1), NEG_BIG, F32), jnp.zeros((tq, 1), F32), jnp.zeros((tq, FOX_HEAD), F32))
    n_full = i * r
    carry = lax.fori_loop(0, n_full, lambda cidx, cr: step(cidx, cr, None), carry)
    for d in range(r):
        carry = step(n_full + d, carry, d * tk)
    _, l, acc = carry
    gate = jax.nn.sigmoid(og_ref[...].astype(F32))
    o_ref[...] = (acc * (1.0 / l) * gate).astype(o_ref.dtype)


def _fox_attention(qg, kv, ck, D):
    S = qg.shape[0]
    H = D // FOX_HEAD
    tk = ck.shape[2]
    tq = tk
    return pl.pallas_call(
        functools.partial(_fox_body, tq=tq, tk=tk),
        grid=(H, S // tq),
        in_specs=[
            pl.BlockSpec((tq, FOX_HEAD), lambda h, i: (i, h)),
            pl.BlockSpec((S, FOX_HEAD), lambda h, i: (0, h)),
            pl.BlockSpec((S, FOX_HEAD), lambda h, i: (0, H + h)),
            pl.BlockSpec((None, S // tk, tk), lambda h, i: (h, 0, 0)),
            pl.BlockSpec((tq, FOX_HEAD), lambda h, i: (i, H + h)),
        ],
        out_specs=pl.BlockSpec((tq, FOX_HEAD), lambda h, i: (i, h)),
        out_shape=jax.ShapeDtypeStruct((S, D), BF16),
        compiler_params=_params("arbitrary", "arbitrary"),
        name="fox_attention",
    )(qg, kv, kv, ck, qg)


def _pad_cols(w, n):
    return jnp.pad(w, ((0, 0), (0, n - w.shape[1])))


def _mlp(x, gain, shift, scale, gate, w_up, w_down, tag):
    h = _norm_mod(x, gain, shift, scale)
    u = _matmul(h, w_up.astype(BF16), out_dtype=BF16, epilogue=_ep_sqrelu, name=f"mlp_up_{tag}")
    return _matmul_resid(u, w_down.astype(BF16), x, gate, name=f"mlp_down_{tag}")


def kernel(x, c, ada_w, ada_b, norm_mix, norm_mlp, w_mlp_up, w_mlp_down, gla_w_in, gla_w_alpha_up,
           gla_b_alpha, gla_o_norm, gla_w_out, kv_ada_w, kv_ada_b, kv_norm, kv_w, fox_b_f, fox_k_norm,
           fox_w_in, fox_q_norm, fox_w_out):
    B, S, D = x.shape
    assert B == 1 and ada_w.shape[0] == 2
    xs = x[0]
    c_col = c.reshape(D, 1)
    KW = (D // GLA_DV) * GLA_DK
    H_FOX = D // FOX_HEAD

    sh1, sc1, g1, sh2, sc2, g2 = jnp.split(_ada(c_col, ada_w, ada_b[0][None], 0), 6, axis=1)
    h = _norm_mod(xs, norm_mix[0][None], sh1, sc1)
    w_in = gla_w_in[0]
    n_main = 2 * KW + 2 * D
    proj = _matmul(h, w_in[:, :n_main].astype(BF16), out_dtype=BF16, name="gla_in")
    a_lo = _matmul(h, _pad_cols(w_in[:, n_main:], LANES).astype(BF16), out_dtype=F32, name="gla_in_lowrank")
    w_up = jnp.pad(gla_w_alpha_up[0], ((0, LANES - gla_w_alpha_up.shape[1]), (0, 0))).astype(BF16)
    y = _gla(proj, a_lo, w_up, gla_b_alpha[0][None], gla_o_norm[0][None], D)
    xs = _matmul_resid(y, gla_w_out[0].astype(BF16), xs, g1, name="gla_out")
    xs = _mlp(xs, norm_mlp[0][None], sh2, sc2, g2, w_mlp_up[0], w_mlp_down[0], "l0")

    kshift, kscale = jnp.split(_ada(c_col, kv_ada_w, kv_ada_b[None], 0), 2, axis=1)
    hk = _norm_mod(xs, kv_norm[None], kshift, kscale)
    kv = _matmul_headnorm(hk, kv_w[:, :2 * D].astype(BF16), fox_k_norm[None], D, 1.0, name="kv_proj")
    f_logit = _matmul(hk, _pad_cols(kv_w[:, 2 * D:], LANES).astype(BF16), out_dtype=F32, name="kv_forget")
    cum_t = _forget_cumsum(f_logit, _pad_cols(fox_b_f[None], LANES))
    tk = min(512, S)
    ck = cum_t.reshape(LANES, S // tk, tk)

    sh1, sc1, g1, sh2, sc2, g2 = jnp.split(_ada(c_col, ada_w, ada_b[1][None], 1), 6, axis=1)
    h = _norm_mod(xs, norm_mix[1][None], sh1, sc1)
    qg = _matmul_headnorm(h, fox_w_in[0].astype(BF16), fox_q_norm[0][None], D, FOX_HEAD ** -0.5, name="fox_in")
    o = _fox_attention(qg, kv, ck, D)
    xs = _matmul_resid(o, fox_w_out[0].astype(BF16), xs, g1, name="fox_out")
    xs = _mlp(xs, norm_mlp[1][None], sh2, sc2, g2, w_mlp_up[1], w_mlp_down[1], "l1")
    return xs[None]
```

```python
import functools

import numpy as np
import jax
import jax.numpy as jnp
from jax import lax
from jax.experimental import pallas as pl
from jax.experimental.pallas import tpu as pltpu

EPS = 1e-6
LANES = 128
GLA_DK = 256
GLA_DV = 512
GLA_TAU = 16.0
GLA_CHUNK = 128
FOX_HEAD = 128
FOX_BLOCK = 1024
FOX_SUB = 256
NEG_BIG = -1e30
VMEM_LIMIT_BYTES = 56 * 1024 * 1024
MM_TILE_M = 1024
MM_TILE_K = 2048
MM_FULL_K = 4096
MM_VMEM_BUDGET = 40 * 1024 * 1024
LOG2E = 1.4426950408889634

F32 = jnp.float32
BF16 = jnp.bfloat16


def _params(*sem):
    return pltpu.CompilerParams(dimension_semantics=sem, vmem_limit_bytes=VMEM_LIMIT_BYTES)


def _log_sigmoid(x):
    return jnp.minimum(x, 0.0) - jnp.log1p(jnp.exp(-jnp.abs(x)))


def _silu(x):
    return x * jax.nn.sigmoid(x)


def _ada_body(c_ref, w_ref, b_ref, o_ref, cb_ref, *, tn):
    @pl.when(pl.program_id(0) == 0)
    def _():
        cc = c_ref[...]
        cb_ref[...] = jnp.broadcast_to(_silu(cc), cb_ref.shape)
    cb = cb_ref[...]
    parts = []
    for a in range(tn // LANES):
        parts.append(jnp.sum(w_ref[:, a * LANES:(a + 1) * LANES] * cb, axis=0, keepdims=True))
    o_ref[...] = jnp.concatenate(parts, axis=1) + b_ref[...]


def _ada(c_col, w, b_row, layer):
    D = c_col.shape[0]
    N = w.shape[-1]
    tn = min(512, N)
    if w.ndim == 3:
        w_spec = pl.BlockSpec((None, D, tn), lambda j: (layer, 0, j))
    else:
        w_spec = pl.BlockSpec((D, tn), lambda j: (0, j))
    return pl.pallas_call(
        functools.partial(_ada_body, tn=tn),
        grid=(N // tn,),
        in_specs=[pl.BlockSpec((D, 1), lambda j: (0, 0)), w_spec,
                  pl.BlockSpec((1, tn), lambda j: (0, j))],
        out_specs=pl.BlockSpec((1, tn), lambda j: (0, j)),
        out_shape=jax.ShapeDtypeStruct((1, N), F32),
        scratch_shapes=[pltpu.VMEM((D, LANES), F32)],
        compiler_params=_params("arbitrary"),
        name="ada_gemv",
    )(c_col, w, b_row)


def _normmod_body(x_ref, g_ref, sh_ref, sc_ref, o_ref):
    x = x_ref[...]
    ms = jnp.mean(x * x, axis=-1, keepdims=True)
    y = x * lax.rsqrt(ms + EPS) * g_ref[...]
    o_ref[...] = (y * (1.0 + sc_ref[...]) + sh_ref[...]).astype(o_ref.dtype)


def _norm_mod(x, gain, shift, scale):
    S, D = x.shape
    tr = min(256, S)
    row = pl.BlockSpec((1, D), lambda i: (0, 0))
    return pl.pallas_call(
        _normmod_body,
        grid=(S // tr,),
        in_specs=[pl.BlockSpec((tr, D), lambda i: (i, 0)), row, row, row],
        out_specs=pl.BlockSpec((tr, D), lambda i: (i, 0)),
        out_shape=jax.ShapeDtypeStruct((S, D), BF16),
        compiler_params=_params("parallel"),
        name="norm_mod",
    )(x, gain, shift, scale)


def _ep_cast(acc, extra, o_ref):
    o_ref[...] = acc.astype(o_ref.dtype)


def _ep_sqrelu(acc, extra, o_ref):
    r = jnp.maximum(acc, 0.0)
    o_ref[...] = (r * r).astype(o_ref.dtype)


def _ep_resid(acc, extra, o_ref):
    x_ref, g_ref = extra
    o_ref[...] = x_ref[...] + g_ref[...] * acc


def _ep_headnorm(acc, extra, o_ref, *, n_norm_blocks, scale):
    (gain_ref,) = extra
    j = pl.program_id(1)

    @pl.when(j < n_norm_blocks)
    def _():
        gain = gain_ref[...]
        for g in range(acc.shape[1] // FOX_HEAD):
            blk = acc[:, g * FOX_HEAD:(g + 1) * FOX_HEAD]
            ms = jnp.mean(blk * blk, axis=-1, keepdims=True)
            y = blk * lax.rsqrt(ms + EPS) * gain * scale
            o_ref[:, g * FOX_HEAD:(g + 1) * FOX_HEAD] = y.astype(o_ref.dtype)

    @pl.when(j >= n_norm_blocks)
    def _():
        o_ref[...] = acc.astype(o_ref.dtype)


def _mm_body(*refs, nk, n_extra, epilogue):
    a_ref, b_ref = refs[0], refs[1]
    extra = refs[2:2 + n_extra]
    o_ref = refs[2 + n_extra]
    if nk == 1:
        epilogue(jnp.dot(a_ref[...], b_ref[...], preferred_element_type=F32), extra, o_ref)
        return
    acc_ref = refs[3 + n_extra]
    k = pl.program_id(2)

    @pl.when(k == 0)
    def _():
        acc_ref[...] = jnp.zeros_like(acc_ref)

    acc_ref[...] += jnp.dot(a_ref[...], b_ref[...], preferred_element_type=F32)

    @pl.when(k == nk - 1)
    def _():
        epilogue(acc_ref[...], extra, o_ref)


def _mm_tiles(M, N, K, out_dtype, resid=False):
    tm = min(MM_TILE_M, M)
    tk = K if K <= MM_FULL_K else MM_TILE_K
    nk = K // tk
    out_bytes = jnp.dtype(out_dtype).itemsize
    for tn in (1024, 512, 256, LANES):
        if tn > N or N % tn:
            continue
        need = 2 * (tm * tk + tk * tn) * 2 + 2 * tm * tn * out_bytes
        need += (2 * tm * tn * 4 if resid else 0) + (tm * tn * 4 if nk > 1 else 0)
        if need <= MM_VMEM_BUDGET:
            break
    return tm, tn, tk


def _matmul(a, b, *, out_dtype, epilogue=_ep_cast, extras=(), extra_specs=(), tiles=None, name):
    M, K = a.shape
    N = b.shape[1]
    tm, tn, tk = tiles or _mm_tiles(M, N, K, out_dtype)
    nk = K // tk
    scratch = [pltpu.VMEM((tm, tn), F32)] if nk > 1 else []
    return pl.pallas_call(
        functools.partial(_mm_body, nk=nk, n_extra=len(extras), epilogue=epilogue),
        grid=(M // tm, N // tn, nk),
        in_specs=[pl.BlockSpec((tm, tk), lambda i, j, k: (i, k)),
                  pl.BlockSpec((tk, tn), lambda i, j, k: (k, j)), *extra_specs],
        out_specs=pl.BlockSpec((tm, tn), lambda i, j, k: (i, j)),
        out_shape=jax.ShapeDtypeStruct((M, N), out_dtype),
        scratch_shapes=scratch,
        compiler_params=_params("parallel", "parallel", "arbitrary"),
        name=name,
    )(a, b, *extras)


def _matmul_resid(a, b, x, gate, name):
    M, N = x.shape
    tm, tn, _ = tiles = _mm_tiles(M, N, a.shape[1], F32, resid=True)
    return _matmul(a, b, out_dtype=F32, epilogue=_ep_resid, extras=(x, gate),
                   extra_specs=(pl.BlockSpec((tm, tn), lambda i, j, k: (i, j)),
                                pl.BlockSpec((1, tn), lambda i, j, k: (0, j))), tiles=tiles, name=name)


def _matmul_headnorm(a, b, gain, n_norm_cols, scale, name):
    tiles = _mm_tiles(a.shape[0], b.shape[1], a.shape[1], BF16)
    ep = functools.partial(_ep_headnorm, n_norm_blocks=n_norm_cols // tiles[1], scale=scale)
    return _matmul(a, b, out_dtype=BF16, epilogue=ep, extras=(gain,),
                   extra_specs=(pl.BlockSpec((1, FOX_HEAD), lambda i, j, k: (0, 0)),), tiles=tiles, name=name)


def _gla_prefix_matrix(C):
    L = int(np.log2(C))
    t = np.arange(C)[:, None]
    j = np.arange(C)[None, :]
    mats = []
    for l in range(L):
        n = 1 << l
        mid = ((t >> (l + 1)) << (l + 1)) + n
        upper = ((t >> l) & 1) == 1
        mats.append(np.where(upper, (j >= mid) & (j <= t), (j > t) & (j < mid)))
    mats.append(j <= t)
    mats.append(j > t)
    return np.concatenate(mats, axis=0).astype(np.float32)


def _gla_body(q_ref, k_ref, v_ref, r_ref, alo_ref, wup_ref, balpha_ref, onorm_ref, f_ref,
              o_ref, state_ref, *, C, L):
    c = pl.program_id(0)
    h = pl.program_id(1)

    @pl.when(c == 0)
    def _():
        state_ref[h] = jnp.zeros(state_ref.shape[1:], F32)

    q = q_ref[...].astype(F32) * (GLA_DK ** -0.5)
    k = k_ref[...].astype(F32)
    v = v_ref[...]

    z = jnp.dot(alo_ref[...].astype(BF16), wup_ref[...], preferred_element_type=F32) + balpha_ref[...]
    la = _log_sigmoid(z) * (1.0 / GLA_TAU)
    hi = la.astype(BF16)
    mid = (la - hi.astype(F32)).astype(BF16)
    xs = jnp.dot(f_ref[...], jnp.concatenate([hi, mid], axis=1), preferred_element_type=F32)
    xs = xs[:, :GLA_DK] + xs[:, GLA_DK:]

    row = lax.broadcasted_iota(jnp.int32, (C, C), 0)
    col = lax.broadcasted_iota(jnp.int32, (C, C), 1)
    scores = jnp.where(row == col, jnp.sum(q * k, axis=-1, keepdims=True), 0.0)
    for l in range(L):
        e = jnp.exp(xs[l * C:(l + 1) * C])
        s_l = lax.dot_general((q * e).astype(BF16), (k * e).astype(BF16),
                              (((1,), (1,)), ((), ())), preferred_element_type=F32)
        rl = row >> l
        cl = col >> l
        scores = jnp.where(((rl ^ cl) == 1) & (rl > cl), s_l, scores)

    b = xs[L * C:(L + 1) * C]
    rem = xs[(L + 1) * C:]

    st = state_ref[h]
    o = jnp.dot(scores.astype(BF16), v, preferred_element_type=F32)
    o += jnp.dot((q * jnp.exp(b)).astype(BF16), st.astype(BF16), preferred_element_type=F32)

    kt = jnp.transpose(k * jnp.exp(rem)).astype(BF16)
    d_row = jnp.exp(b[C - 1:C, :])
    d_col = jnp.transpose(jnp.broadcast_to(d_row, (LANES, GLA_DK)))[:, :1]
    state_ref[h] = d_col * st + jnp.dot(kt, v, preferred_element_type=F32)

    ms = jnp.mean(o * o, axis=-1, keepdims=True)
    y = o * lax.rsqrt(ms + EPS) * onorm_ref[...]
    o_ref[...] = (y * _silu(r_ref[...].astype(F32))).astype(o_ref.dtype)


def _gla(proj, a_lo, w_up, b_alpha, o_norm, D):
    S = proj.shape[0]
    H = D // GLA_DV
    KW = H * GLA_DK
    C = min(GLA_CHUNK, S)
    L = int(np.log2(C))
    fmat = jnp.asarray(_gla_prefix_matrix(C), BF16)
    kq, kv, kr = KW // GLA_DK, (2 * KW) // GLA_DV, (2 * KW + D) // GLA_DV
    return pl.pallas_call(
        functools.partial(_gla_body, C=C, L=L),
        grid=(S // C, H),
        in_specs=[
            pl.BlockSpec((C, GLA_DK), lambda c, h: (c, h)),
            pl.BlockSpec((C, GLA_DK), lambda c, h: (c, kq + h)),
            pl.BlockSpec((C, GLA_DV), lambda c, h: (c, kv + h)),
            pl.BlockSpec((C, GLA_DV), lambda c, h: (c, kr + h)),
            pl.BlockSpec((C, LANES), lambda c, h: (c, 0)),
            pl.BlockSpec((LANES, GLA_DK), lambda c, h: (0, h)),
            pl.BlockSpec((1, GLA_DK), lambda c, h: (0, h)),
            pl.BlockSpec((1, GLA_DV), lambda c, h: (0, 0)),
            pl.BlockSpec(fmat.shape, lambda c, h: (0, 0)),
        ],
        out_specs=pl.BlockSpec((C, GLA_DV), lambda c, h: (c, h)),
        out_shape=jax.ShapeDtypeStruct((S, D), BF16),
        scratch_shapes=[pltpu.VMEM((H, GLA_DK, GLA_DV), F32)],
        compiler_params=_params("arbitrary", "arbitrary"),
        name="gla_chunk",
    )(proj, proj, proj, proj, a_lo, w_up, b_alpha, o_norm, fmat)


def _split3(x):
    hi = x.astype(BF16)
    r1 = x - hi.astype(F32)
    mid = r1.astype(BF16)
    lo = (r1 - mid.astype(F32)).astype(BF16)
    return hi, mid, lo


def _kvaug_body(f_ref, bf_ref, tri_ref, k_ref, v_ref, ka_ref, va_ref, carry_ref, *, T, H):
    @pl.when(pl.program_id(0) == 0)
    def _():
        carry_ref[...] = jnp.zeros_like(carry_ref)

    lf = _log_sigmoid(f_ref[...] + bf_ref[...])
    cs = jnp.dot(tri_ref[...], jnp.concatenate(_split3(lf), axis=1), preferred_element_type=F32)
    cum = cs[:, :LANES] + cs[:, LANES:2 * LANES] + cs[:, 2 * LANES:] + carry_ref[...]
    carry_ref[...] = cum[T - 1:T, :]
    lane = lax.broadcasted_iota(jnp.int32, (T, LANES), 1)
    v_tail = jnp.where(lane == 0, 1.0, 0.0).astype(BF16)
    for h in range(H):
        val = cum[:, h:h + 1] * (-LOG2E)
        hi = val.astype(BF16).astype(F32)
        mid = (val - hi).astype(BF16).astype(F32)
        lo = val - hi - mid
        k_tail = jnp.where(lane == 0, hi, jnp.where(lane == 1, mid, jnp.where(lane == 2, lo, 0.0)))
        head = slice(h * FOX_HEAD, (h + 1) * FOX_HEAD)
        ka_ref[:, 2 * h * LANES:(2 * h + 1) * LANES] = k_ref[:, head]
        ka_ref[:, (2 * h + 1) * LANES:(2 * h + 2) * LANES] = k_tail.astype(BF16)
        va_ref[:, 2 * h * LANES:(2 * h + 1) * LANES] = v_ref[:, head]
        va_ref[:, (2 * h + 1) * LANES:(2 * h + 2) * LANES] = v_tail


def _fox_kv_augment(f_logit, b_f, kv, D):
    S = f_logit.shape[0]
    H = D // FOX_HEAD
    T = min(256, S)
    tri = jnp.asarray(np.tril(np.ones((T, T), np.float32)), BF16)
    aug = jax.ShapeDtypeStruct((S, 2 * D), BF16)
    return pl.pallas_call(
        functools.partial(_kvaug_body, T=T, H=H),
        grid=(S // T,),
        in_specs=[pl.BlockSpec((T, LANES), lambda i: (i, 0)),
                  pl.BlockSpec((1, LANES), lambda i: (0, 0)),
                  pl.BlockSpec((T, T), lambda i: (0, 0)),
                  pl.BlockSpec((T, D), lambda i: (i, 0)),
                  pl.BlockSpec((T, D), lambda i: (i, 1))],
        out_specs=[pl.BlockSpec((T, 2 * D), lambda i: (i, 0)), pl.BlockSpec((T, 2 * D), lambda i: (i, 0))],
        out_shape=[aug, aug],
        scratch_shapes=[pltpu.VMEM((1, LANES), F32)],
        compiler_params=_params("arbitrary"),
        name="fox_kv_augment",
    )(f_logit, b_f, tri, kv, kv)


def _fox_body(q_ref, k_ref, v_ref, og_ref, o_ref, qa_ref, sa_ref, sb_ref, m_ref, acc_ref, *, tq, tk, tr):
    i = pl.program_id(1)
    nsub = tq // tr
    lane_q = lax.broadcasted_iota(jnp.int32, (tq, LANES), 1)
    qa_ref[:, :FOX_HEAD] = q_ref[...]
    qa_ref[:, FOX_HEAD:] = jnp.where(lane_q < 3, 1.0, 0.0).astype(BF16)
    m_ref[...] = jnp.full(m_ref.shape, NEG_BIG, F32)
    acc_ref[...] = jnp.zeros(acc_ref.shape, F32)

    def logits_to(cidx, s_ref):
        kc = k_ref[pl.ds(pl.multiple_of(cidx * tk, tk), tk), :]
        for r in range(nsub):
            rows = slice(r * tr, (r + 1) * tr)
            s_ref[rows, :] = lax.dot_general(qa_ref[rows, :], kc, (((1,), (1,)), ((), ())),
                                             preferred_element_type=F32)

    def softmax_pv(cidx, s_ref, masked):
        vc = v_ref[pl.ds(pl.multiple_of(cidx * tk, tk), tk), :]
        for r in range(nsub):
            rows = slice(r * tr, (r + 1) * tr)
            s = s_ref[rows, :]
            if masked:
                row = lax.broadcasted_iota(jnp.int32, (tr, tk), 0) + r * tr
                col = lax.broadcasted_iota(jnp.int32, (tr, tk), 1)
                s = jnp.where(row >= col, s, NEG_BIG)
            m_old = m_ref[rows, :]
            m_new = jnp.maximum(m_old, jnp.max(s, axis=-1, keepdims=True))
            m_ref[rows, :] = m_new
            p = jnp.exp2(s - m_new).astype(BF16)
            acc_ref[rows, :] = (jnp.exp2(m_old - m_new) * acc_ref[rows, :]
                                + jnp.dot(p, vc, preferred_element_type=F32))

    def finalize():
        for r in range(nsub):
            rows = slice(r * tr, (r + 1) * tr)
            gate = jax.nn.sigmoid(og_ref[rows, :].astype(F32))
            inv_l = 1.0 / acc_ref[rows, FOX_HEAD:FOX_HEAD + 1]
            o_ref[rows, :] = (acc_ref[rows, :FOX_HEAD] * inv_l * gate).astype(o_ref.dtype)

    def pair(j, carry):
        c = 2 * j
        logits_to(c + 1, sb_ref)
        softmax_pv(c, sa_ref, False)
        logits_to(c + 2, sa_ref)
        softmax_pv(c + 1, sb_ref, False)
        return carry

    logits_to(0, sa_ref)
    lax.fori_loop(0, i // 2, pair, 0)

    @pl.when(i % 2 == 0)
    def _():
        softmax_pv(i, sa_ref, True)
        finalize()

    @pl.when(i % 2 == 1)
    def _():
        logits_to(i, sb_ref)
        softmax_pv(i - 1, sa_ref, False)
        softmax_pv(i, sb_ref, True)
        finalize()


def _fox_attention(qg, k_aug, v_aug, D):
    S = qg.shape[0]
    H = D // FOX_HEAD
    tq = tk = min(FOX_BLOCK, S)
    return pl.pallas_call(
        functools.partial(_fox_body, tq=tq, tk=tk, tr=min(FOX_SUB, tq)),
        grid=(H, S // tq),
        in_specs=[
            pl.BlockSpec((tq, FOX_HEAD), lambda h, i: (i, h)),
            pl.BlockSpec((S, 2 * LANES), lambda h, i: (0, h)),
            pl.BlockSpec((S, 2 * LANES), lambda h, i: (0, h)),
            pl.BlockSpec((tq, FOX_HEAD), lambda h, i: (i, H + h)),
        ],
        out_specs=pl.BlockSpec((tq, FOX_HEAD), lambda h, i: (i, h)),
        out_shape=jax.ShapeDtypeStruct((S, D), BF16),
        scratch_shapes=[pltpu.VMEM((tq, 2 * LANES), BF16), pltpu.VMEM((tq, tk), F32), pltpu.VMEM((tq, tk), F32),
                        pltpu.VMEM((tq, 1), F32), pltpu.VMEM((tq, 2 * FOX_HEAD), F32)],
        compiler_params=_params("arbitrary", "arbitrary"),
        name="fox_attention",
    )(qg, k_aug, v_aug, qg)


def _pad_cols(w, n):
    return jnp.pad(w, ((0, 0), (0, n - w.shape[1])))


def _mlp(x, gain, shift, scale, gate, w_up, w_down, tag):
    h = _norm_mod(x, gain, shift, scale)
    u = _matmul(h, w_up.astype(BF16), out_dtype=BF16, epilogue=_ep_sqrelu, name=f"mlp_up_{tag}")
    return _matmul_resid(u, w_down.astype(BF16), x, gate, name=f"mlp_down_{tag}")


def kernel(x, c, ada_w, ada_b, norm_mix, norm_mlp, w_mlp_up, w_mlp_down, gla_w_in, gla_w_alpha_up,
           gla_b_alpha, gla_o_norm, gla_w_out, kv_ada_w, kv_ada_b, kv_norm, kv_w, fox_b_f, fox_k_norm,
           fox_w_in, fox_q_norm, fox_w_out):
    B, S, D = x.shape
    assert B == 1 and ada_w.shape[0] == 2
    xs = x[0]
    c_col = c.reshape(D, 1)
    KW = (D // GLA_DV) * GLA_DK
    H_FOX = D // FOX_HEAD

    sh1, sc1, g1, sh2, sc2, g2 = jnp.split(_ada(c_col, ada_w, ada_b[0][None], 0), 6, axis=1)
    h = _norm_mod(xs, norm_mix[0][None], sh1, sc1)
    w_in = gla_w_in[0]
    n_main = 2 * KW + 2 * D
    proj = _matmul(h, w_in[:, :n_main].astype(BF16), out_dtype=BF16, name="gla_in")
    a_lo = _matmul(h, _pad_cols(w_in[:, n_main:], LANES).astype(BF16), out_dtype=F32, name="gla_in_lowrank")
    w_up = jnp.pad(gla_w_alpha_up[0], ((0, LANES - gla_w_alpha_up.shape[1]), (0, 0))).astype(BF16)
    y = _gla(proj, a_lo, w_up, gla_b_alpha[0][None], gla_o_norm[0][None], D)
    xs = _matmul_resid(y, gla_w_out[0].astype(BF16), xs, g1, name="gla_out")
    xs = _mlp(xs, norm_mlp[0][None], sh2, sc2, g2, w_mlp_up[0], w_mlp_down[0], "l0")

    kshift, kscale = jnp.split(_ada(c_col, kv_ada_w, kv_ada_b[None], 0), 2, axis=1)
    hk = _norm_mod(xs, kv_norm[None], kshift, kscale)
    kv = _matmul_headnorm(hk, kv_w[:, :2 * D].astype(BF16), fox_k_norm[None], D, 1.0, name="kv_proj")
    f_logit = _matmul(hk, _pad_cols(kv_w[:, 2 * D:], LANES).astype(BF16), out_dtype=F32, name="kv_forget")
    k_aug, v_aug = _fox_kv_augment(f_logit, _pad_cols(fox_b_f[None], LANES), kv, D)

    sh1, sc1, g1, sh2, sc2, g2 = jnp.split(_ada(c_col, ada_w, ada_b[1][None], 1), 6, axis=1)
    h = _norm_mod(xs, norm_mix[1][None], sh1, sc1)
    qg = _matmul_headnorm(h, fox_w_in[0].astype(BF16), fox_q_norm[0][None], D, FOX_HEAD ** -0.5 * LOG2E,
                          name="fox_in")
    o = _fox_attention(qg, k_aug, v_aug, D)
    xs = _matmul_resid(o, fox_w_out[0].astype(BF16), xs, g1, name="fox_out")
    xs = _mlp(xs, norm_mlp[1][None], sh2, sc2, g2, w_mlp_up[1], w_mlp_down[1], "l1")
    return xs[None]
```

```python
import functools
from typing import NamedTuple

import numpy as np
import jax
import jax.numpy as jnp
from jax import lax
from jax.experimental import pallas as pl
from jax.experimental.pallas import tpu as pltpu

EPS = 1e-6
LANES = 128
GLA_DK = 256
GLA_DV = 512
GLA_TAU = 16.0
GLA_CHUNK = 128
GLA_HEADS_PER_STEP = 8
FOX_HEAD = 128
FOX_BLOCK = 1024
FOX_SUB = 256
NEG_BIG = -1e30
VMEM_LIMIT_BYTES = 56 * 1024 * 1024
MM_TILE_M = 1024
MM_TILE_K = 2048
MM_FULL_K = 4096
MM_VMEM_BUDGET = 40 * 1024 * 1024
LOG2E = 1.4426950408889634

F32 = jnp.float32
BF16 = jnp.bfloat16


def _params(*sem):
    return pltpu.CompilerParams(dimension_semantics=sem, vmem_limit_bytes=VMEM_LIMIT_BYTES)


def _log_sigmoid(x):
    return jnp.minimum(x, 0.0) - jnp.log1p(jnp.exp(-jnp.abs(x)))


def _silu(x):
    return x * jax.nn.sigmoid(x)


def _ada_body(c_ref, w_ref, b_ref, o_ref, cb_ref, *, tn):
    @pl.when(pl.program_id(0) == 0)
    def _():
        cc = c_ref[...]
        cb_ref[...] = jnp.broadcast_to(_silu(cc), cb_ref.shape)
    cb = cb_ref[...]
    parts = []
    for a in range(tn // LANES):
        parts.append(jnp.sum(w_ref[:, a * LANES:(a + 1) * LANES] * cb, axis=0, keepdims=True))
    o_ref[...] = jnp.concatenate(parts, axis=1) + b_ref[...]


def _ada(c_col, w, b_row, layer):
    D = c_col.shape[0]
    N = w.shape[-1]
    tn = min(512, N)
    if w.ndim == 3:
        w_spec = pl.BlockSpec((None, D, tn), lambda j: (layer, 0, j))
    else:
        w_spec = pl.BlockSpec((D, tn), lambda j: (0, j))
    return pl.pallas_call(
        functools.partial(_ada_body, tn=tn),
        grid=(N // tn,),
        in_specs=[pl.BlockSpec((D, 1), lambda j: (0, 0)), w_spec,
                  pl.BlockSpec((1, tn), lambda j: (0, j))],
        out_specs=pl.BlockSpec((1, tn), lambda j: (0, j)),
        out_shape=jax.ShapeDtypeStruct((1, N), F32),
        scratch_shapes=[pltpu.VMEM((D, LANES), F32)],
        compiler_params=_params("arbitrary"),
        name="ada_gemv",
    )(c_col, w, b_row)


def _normmod_body(x_ref, g_ref, sh_ref, sc_ref, o_ref):
    x = x_ref[...]
    ms = jnp.mean(x * x, axis=-1, keepdims=True)
    y = x * lax.rsqrt(ms + EPS) * g_ref[...]
    o_ref[...] = (y * (1.0 + sc_ref[...]) + sh_ref[...]).astype(o_ref.dtype)


def _norm_mod(x, gain, shift, scale):
    S, D = x.shape
    tr = min(256, S)
    row = pl.BlockSpec((1, D), lambda i: (0, 0))
    return pl.pallas_call(
        _normmod_body,
        grid=(S // tr,),
        in_specs=[pl.BlockSpec((tr, D), lambda i: (i, 0)), row, row, row],
        out_specs=pl.BlockSpec((tr, D), lambda i: (i, 0)),
        out_shape=jax.ShapeDtypeStruct((S, D), BF16),
        compiler_params=_params("parallel"),
        name="norm_mod",
    )(x, gain, shift, scale)


def _ep_cast(acc, extra, o_ref):
    o_ref[...] = acc.astype(o_ref.dtype)


def _ep_sqrelu(acc, extra, o_ref):
    r = jnp.maximum(acc, 0.0)
    o_ref[...] = (r * r).astype(o_ref.dtype)


def _ep_resid(acc, extra, o_ref):
    x_ref, g_ref = extra
    o_ref[...] = x_ref[...] + g_ref[...] * acc


def _ep_headnorm(acc, extra, o_ref, *, n_norm_blocks, scale):
    (gain_ref,) = extra
    j = pl.program_id(1)

    @pl.when(j < n_norm_blocks)
    def _():
        gain = gain_ref[...]
        for g in range(acc.shape[1] // FOX_HEAD):
            blk = acc[:, g * FOX_HEAD:(g + 1) * FOX_HEAD]
            ms = jnp.mean(blk * blk, axis=-1, keepdims=True)
            y = blk * lax.rsqrt(ms + EPS) * gain * scale
            o_ref[:, g * FOX_HEAD:(g + 1) * FOX_HEAD] = y.astype(o_ref.dtype)

    @pl.when(j >= n_norm_blocks)
    def _():
        o_ref[...] = acc.astype(o_ref.dtype)


def _mm_body(*refs, nk, n_extra, epilogue):
    a_ref, b_ref = refs[0], refs[1]
    extra = refs[2:2 + n_extra]
    o_ref = refs[2 + n_extra]
    if nk == 1:
        epilogue(jnp.dot(a_ref[...], b_ref[...].astype(BF16), preferred_element_type=F32), extra, o_ref)
        return
    acc_ref = refs[3 + n_extra]
    k = pl.program_id(2)

    @pl.when(k == 0)
    def _():
        acc_ref[...] = jnp.zeros_like(acc_ref)

    acc_ref[...] += jnp.dot(a_ref[...], b_ref[...].astype(BF16), preferred_element_type=F32)

    @pl.when(k == nk - 1)
    def _():
        epilogue(acc_ref[...], extra, o_ref)


class _Weight(NamedTuple):
    array: jax.Array
    layer: int | None = None
    n_cols: int | None = None

    @property
    def shape(self):
        return self.array.shape[-2], self.n_cols or self.array.shape[-1]


def _mm_tiles(M, w, out_dtype, resid=False):
    K, N = w.shape
    tm = min(MM_TILE_M, M)
    tk = K if K <= MM_FULL_K else MM_TILE_K
    nk = K // tk
    out_bytes = jnp.dtype(out_dtype).itemsize
    for tn in (1024, 512, 256, LANES):
        if tn > N or N % tn:
            continue
        need = 2 * tm * tk * 2 + 2 * tk * tn * w.array.dtype.itemsize + 2 * tm * tn * out_bytes
        need += (2 * tm * tn * 4 if resid else 0) + (tm * tn * 4 if nk > 1 else 0)
        if need <= MM_VMEM_BUDGET:
            break
    return tm, tn, tk


def _matmul(a, w, *, out_dtype, epilogue=_ep_cast, extras=(), extra_specs=(), tiles=None, name):
    M, K = a.shape
    N = w.shape[1]
    tm, tn, tk = tiles or _mm_tiles(M, w, out_dtype)
    nk = K // tk
    scratch = [pltpu.VMEM((tm, tn), F32)] if nk > 1 else []
    if w.array.ndim == 3:
        w_spec = pl.BlockSpec((None, tk, tn), lambda i, j, k: (w.layer, k, j))
    else:
        w_spec = pl.BlockSpec((tk, tn), lambda i, j, k: (k, j))
    return pl.pallas_call(
        functools.partial(_mm_body, nk=nk, n_extra=len(extras), epilogue=epilogue),
        grid=(M // tm, N // tn, nk),
        in_specs=[pl.BlockSpec((tm, tk), lambda i, j, k: (i, k)), w_spec, *extra_specs],
        out_specs=pl.BlockSpec((tm, tn), lambda i, j, k: (i, j)),
        out_shape=jax.ShapeDtypeStruct((M, N), out_dtype),
        scratch_shapes=scratch,
        compiler_params=_params("parallel", "parallel", "arbitrary"),
        name=name,
    )(a, w.array, *extras)


def _matmul_resid(a, w, x, gate, name):
    M, N = x.shape
    tm, tn, _ = tiles = _mm_tiles(M, w, F32, resid=True)
    return _matmul(a, w, out_dtype=F32, epilogue=_ep_resid, extras=(x, gate),
                   extra_specs=(pl.BlockSpec((tm, tn), lambda i, j, k: (i, j)),
                                pl.BlockSpec((1, tn), lambda i, j, k: (0, j))), tiles=tiles, name=name)


def _matmul_headnorm(a, w, gain, n_norm_cols, scale, name):
    tiles = _mm_tiles(a.shape[0], w, BF16)
    ep = functools.partial(_ep_headnorm, n_norm_blocks=n_norm_cols // tiles[1], scale=scale)
    return _matmul(a, w, out_dtype=BF16, epilogue=ep, extras=(gain,),
                   extra_specs=(pl.BlockSpec((1, FOX_HEAD), lambda i, j, k: (0, 0)),), tiles=tiles, name=name)


def _gla_prefix_matrix(C):
    L = int(np.log2(C))
    t = np.arange(C)[:, None]
    j = np.arange(C)[None, :]
    mats = []
    for l in range(L):
        n = 1 << l
        mid = ((t >> (l + 1)) << (l + 1)) + n
        upper = ((t >> l) & 1) == 1
        mats.append(np.where(upper, (j >= mid) & (j <= t), (j > t) & (j < mid)))
    mats.append(j <= t)
    mats.append(j > t)
    return np.concatenate(mats, axis=0).astype(np.float32)


def _gla_body(q_ref, k_ref, v_ref, r_ref, alo_ref, wup_ref, balpha_ref, onorm_ref, f_ref,
              o_ref, state_ref, *, C, L, HP):
    c = pl.program_id(0)
    hb = pl.program_id(1)
    row = lax.broadcasted_iota(jnp.int32, (C, C), 0)
    col = lax.broadcasted_iota(jnp.int32, (C, C), 1)
    a_lo = alo_ref[...].astype(BF16)
    fmat = f_ref[...]

    heads = range(HP)
    kcols = [slice(j * GLA_DK, (j + 1) * GLA_DK) for j in heads]
    vcols = [slice(j * GLA_DV, (j + 1) * GLA_DV) for j in heads]

    @pl.when(c == 0)
    def _():
        for j in heads:
            state_ref[hb * HP + j] = jnp.zeros(state_ref.shape[1:], F32)

    st = [state_ref[hb * HP + j] for j in heads]
    q = [q_ref[:, kcols[j]].astype(F32) * (GLA_DK ** -0.5) for j in heads]
    k = [k_ref[:, kcols[j]].astype(F32) for j in heads]
    v = [v_ref[:, vcols[j]] for j in heads]

    xs = []
    for j in heads:
        z = jnp.dot(a_lo, wup_ref[:, kcols[j]], preferred_element_type=F32) + balpha_ref[:, kcols[j]]
        la = _log_sigmoid(z) * (1.0 / GLA_TAU)
        hi = la.astype(BF16)
        mid = (la - hi.astype(F32)).astype(BF16)
        x = jnp.dot(fmat, jnp.concatenate([hi, mid], axis=1), preferred_element_type=F32)
        xs.append(x[:, :GLA_DK] + x[:, GLA_DK:])

    scores = [jnp.where(row == col, jnp.sum(q[j] * k[j], axis=-1, keepdims=True), 0.0) for j in heads]
    for l in range(L):
        rl = row >> l
        cl = col >> l
        pick = ((rl ^ cl) == 1) & (rl > cl)
        for j in heads:
            e = jnp.exp(xs[j][l * C:(l + 1) * C])
            s_l = lax.dot_general((q[j] * e).astype(BF16), (k[j] * e).astype(BF16),
                                  (((1,), (1,)), ((), ())), preferred_element_type=F32)
            scores[j] = jnp.where(pick, s_l, scores[j])

    outs = []
    for j in heads:
        b = xs[j][L * C:(L + 1) * C]
        rem = xs[j][(L + 1) * C:]
        o = jnp.dot(scores[j].astype(BF16), v[j], preferred_element_type=F32)
        o += jnp.dot((q[j] * jnp.exp(b)).astype(BF16), st[j].astype(BF16), preferred_element_type=F32)
        outs.append(o)
        kt = jnp.transpose(k[j] * jnp.exp(rem)).astype(BF16)
        d_row = jnp.exp(b[C - 1:C, :])
        d_col = jnp.transpose(jnp.broadcast_to(d_row, (LANES, GLA_DK)))[:, :1]
        st[j] = d_col * st[j] + jnp.dot(kt, v[j], preferred_element_type=F32)

    for j in heads:
        state_ref[hb * HP + j] = st[j]
        o = outs[j]
        ms = jnp.mean(o * o, axis=-1, keepdims=True)
        y = o * lax.rsqrt(ms + EPS) * onorm_ref[...]
        o_ref[:, vcols[j]] = (y * _silu(r_ref[:, vcols[j]].astype(F32))).astype(o_ref.dtype)


def _gla(proj, a_lo, w_up, b_alpha, o_norm, D):
    S = proj.shape[0]
    H = D // GLA_DV
    KW = H * GLA_DK
    C = min(GLA_CHUNK, S)
    L = int(np.log2(C))
    HP = min(GLA_HEADS_PER_STEP, H)
    fmat = jnp.asarray(_gla_prefix_matrix(C), BF16)
    wk, wv = HP * GLA_DK, HP * GLA_DV
    kq, kv, kr = KW // wk, (2 * KW) // wv, (2 * KW + D) // wv
    return pl.pallas_call(
        functools.partial(_gla_body, C=C, L=L, HP=HP),
        grid=(S // C, H // HP),
        in_specs=[
            pl.BlockSpec((C, wk), lambda c, h: (c, h)),
            pl.BlockSpec((C, wk), lambda c, h: (c, kq + h)),
            pl.BlockSpec((C, wv), lambda c, h: (c, kv + h)),
            pl.BlockSpec((C, wv), lambda c, h: (c, kr + h)),
            pl.BlockSpec((C, LANES), lambda c, h: (c, 0)),
            pl.BlockSpec((LANES, wk), lambda c, h: (0, h)),
            pl.BlockSpec((1, wk), lambda c, h: (0, h)),
            pl.BlockSpec((1, GLA_DV), lambda c, h: (0, 0)),
            pl.BlockSpec(fmat.shape, lambda c, h: (0, 0)),
        ],
        out_specs=pl.BlockSpec((C, wv), lambda c, h: (c, h)),
        out_shape=jax.ShapeDtypeStruct((S, D), BF16),
        scratch_shapes=[pltpu.VMEM((H, GLA_DK, GLA_DV), F32)],
        compiler_params=_params("arbitrary", "arbitrary"),
        name="gla_chunk",
    )(proj, proj, proj, proj, a_lo, w_up, b_alpha, o_norm, fmat)


def _split3(x):
    hi = x.astype(BF16)
    r1 = x - hi.astype(F32)
    mid = r1.astype(BF16)
    lo = (r1 - mid.astype(F32)).astype(BF16)
    return hi, mid, lo


def _kvaug_body(f_ref, bf_ref, tri_ref, k_ref, v_ref, ka_ref, va_ref, carry_ref, *, T, H):
    @pl.when(pl.program_id(0) == 0)
    def _():
        carry_ref[...] = jnp.zeros_like(carry_ref)

    lf = _log_sigmoid(f_ref[...] + bf_ref[...])
    cs = jnp.dot(tri_ref[...], jnp.concatenate(_split3(lf), axis=1), preferred_element_type=F32)
    cum = cs[:, :LANES] + cs[:, LANES:2 * LANES] + cs[:, 2 * LANES:] + carry_ref[...]
    carry_ref[...] = cum[T - 1:T, :]
    lane = lax.broadcasted_iota(jnp.int32, (T, LANES), 1)
    v_tail = jnp.where(lane == 0, 1.0, 0.0).astype(BF16)
    for h in range(H):
        val = cum[:, h:h + 1] * (-LOG2E)
        hi = val.astype(BF16).astype(F32)
        mid = (val - hi).astype(BF16).astype(F32)
        lo = val - hi - mid
        k_tail = jnp.where(lane == 0, hi, jnp.where(lane == 1, mid, jnp.where(lane == 2, lo, 0.0)))
        head = slice(h * FOX_HEAD, (h + 1) * FOX_HEAD)
        ka_ref[:, 2 * h * LANES:(2 * h + 1) * LANES] = k_ref[:, head]
        ka_ref[:, (2 * h + 1) * LANES:(2 * h + 2) * LANES] = k_tail.astype(BF16)
        va_ref[:, 2 * h * LANES:(2 * h + 1) * LANES] = v_ref[:, head]
        va_ref[:, (2 * h + 1) * LANES:(2 * h + 2) * LANES] = v_tail


def _fox_kv_augment(f_logit, b_f, kv, D):
    S = f_logit.shape[0]
    H = D // FOX_HEAD
    T = min(256, S)
    tri = jnp.asarray(np.tril(np.ones((T, T), np.float32)), BF16)
    aug = jax.ShapeDtypeStruct((S, 2 * D), BF16)
    return pl.pallas_call(
        functools.partial(_kvaug_body, T=T, H=H),
        grid=(S // T,),
        in_specs=[pl.BlockSpec((T, LANES), lambda i: (i, 0)),
                  pl.BlockSpec((1, LANES), lambda i: (0, 0)),
                  pl.BlockSpec((T, T), lambda i: (0, 0)),
                  pl.BlockSpec((T, D), lambda i: (i, 0)),
                  pl.BlockSpec((T, D), lambda i: (i, 1))],
        out_specs=[pl.BlockSpec((T, 2 * D), lambda i: (i, 0)), pl.BlockSpec((T, 2 * D), lambda i: (i, 0))],
        out_shape=[aug, aug],
        scratch_shapes=[pltpu.VMEM((1, LANES), F32)],
        compiler_params=_params("arbitrary"),
        name="fox_kv_augment",
    )(f_logit, b_f, tri, kv, kv)


def _fox_body(q_ref, k_ref, v_ref, og_ref, o_ref, qa_ref, sa_ref, sb_ref, m_ref, acc_ref, *, tq, tk, tr):
    i = pl.program_id(1)
    nsub = tq // tr
    lane_q = lax.broadcasted_iota(jnp.int32, (tq, LANES), 1)
    qa_ref[:, :FOX_HEAD] = q_ref[...]
    qa_ref[:, FOX_HEAD:] = jnp.where(lane_q < 3, 1.0, 0.0).astype(BF16)
    m_ref[...] = jnp.full(m_ref.shape, NEG_BIG, F32)
    acc_ref[...] = jnp.zeros(acc_ref.shape, F32)

    def logits_to(cidx, s_ref):
        kc = k_ref[pl.ds(pl.multiple_of(cidx * tk, tk), tk), :]
        for r in range(nsub):
            rows = slice(r * tr, (r + 1) * tr)
            s_ref[rows, :] = lax.dot_general(qa_ref[rows, :], kc, (((1,), (1,)), ((), ())),
                                             preferred_element_type=F32)

    def softmax_pv(cidx, s_ref, masked):
        vc = v_ref[pl.ds(pl.multiple_of(cidx * tk, tk), tk), :]
        for r in range(nsub):
            rows = slice(r * tr, (r + 1) * tr)
            s = s_ref[rows, :]
            if masked:
                row = lax.broadcasted_iota(jnp.int32, (tr, tk), 0) + r * tr
                col = lax.broadcasted_iota(jnp.int32, (tr, tk), 1)
                s = jnp.where(row >= col, s, NEG_BIG)
            m_old = m_ref[rows, :]
            m_new = jnp.maximum(m_old, jnp.max(s, axis=-1, keepdims=True))
            m_ref[rows, :] = m_new
            p = jnp.exp2(s - m_new).astype(BF16)
            acc_ref[rows, :] = (jnp.exp2(m_old - m_new) * acc_ref[rows, :]
                                + jnp.dot(p, vc, preferred_element_type=F32))

    def finalize():
        for r in range(nsub):
            rows = slice(r * tr, (r + 1) * tr)
            gate = jax.nn.sigmoid(og_ref[rows, :].astype(F32))
            inv_l = 1.0 / acc_ref[rows, FOX_HEAD:FOX_HEAD + 1]
            o_ref[rows, :] = (acc_ref[rows, :FOX_HEAD] * inv_l * gate).astype(o_ref.dtype)

    def pair(j, carry):
        c = 2 * j
        logits_to(c + 1, sb_ref)
        softmax_pv(c, sa_ref, False)
        logits_to(c + 2, sa_ref)
        softmax_pv(c + 1, sb_ref, False)
        return carry

    logits_to(0, sa_ref)
    lax.fori_loop(0, i // 2, pair, 0)

    @pl.when(i % 2 == 0)
    def _():
        softmax_pv(i, sa_ref, True)
        finalize()

    @pl.when(i % 2 == 1)
    def _():
        logits_to(i, sb_ref)
        softmax_pv(i - 1, sa_ref, False)
        softmax_pv(i, sb_ref, True)
        finalize()


def _fox_attention(qg, k_aug, v_aug, D):
    S = qg.shape[0]
    H = D // FOX_HEAD
    tq = tk = min(FOX_BLOCK, S)
    return pl.pallas_call(
        functools.partial(_fox_body, tq=tq, tk=tk, tr=min(FOX_SUB, tq)),
        grid=(H, S // tq),
        in_specs=[
            pl.BlockSpec((tq, FOX_HEAD), lambda h, i: (i, h)),
            pl.BlockSpec((S, 2 * LANES), lambda h, i: (0, h)),
            pl.BlockSpec((S, 2 * LANES), lambda h, i: (0, h)),
            pl.BlockSpec((tq, FOX_HEAD), lambda h, i: (i, H + h)),
        ],
        out_specs=pl.BlockSpec((tq, FOX_HEAD), lambda h, i: (i, h)),
        out_shape=jax.ShapeDtypeStruct((S, D), BF16),
        scratch_shapes=[pltpu.VMEM((tq, 2 * LANES), BF16), pltpu.VMEM((tq, tk), F32), pltpu.VMEM((tq, tk), F32),
                        pltpu.VMEM((tq, 1), F32), pltpu.VMEM((tq, 2 * FOX_HEAD), F32)],
        compiler_params=_params("arbitrary", "arbitrary"),
        name="fox_attention",
    )(qg, k_aug, v_aug, qg)


def _pad_cols(w, n):
    return jnp.pad(w, ((0, 0), (0, n - w.shape[1])))


def _mlp(x, gain, shift, scale, gate, w_up, w_down, layer):
    h = _norm_mod(x, gain, shift, scale)
    u = _matmul(h, _Weight(w_up, layer), out_dtype=BF16, epilogue=_ep_sqrelu, name=f"mlp_up_l{layer}")
    return _matmul_resid(u, _Weight(w_down[layer].astype(BF16)), x, gate, name=f"mlp_down_l{layer}")


def kernel(x, c, ada_w, ada_b, norm_mix, norm_mlp, w_mlp_up, w_mlp_down, gla_w_in, gla_w_alpha_up,
           gla_b_alpha, gla_o_norm, gla_w_out, kv_ada_w, kv_ada_b, kv_norm, kv_w, fox_b_f, fox_k_norm,
           fox_w_in, fox_q_norm, fox_w_out):
    B, S, D = x.shape
    assert B == 1 and ada_w.shape[0] == 2
    xs = x[0]
    c_col = c.reshape(D, 1)
    KW = (D // GLA_DV) * GLA_DK

    sh1, sc1, g1, sh2, sc2, g2 = jnp.split(_ada(c_col, ada_w, ada_b[0][None], 0), 6, axis=1)
    h = _norm_mod(xs, norm_mix[0][None], sh1, sc1)
    n_main = 2 * KW + 2 * D
    proj = _matmul(h, _Weight(gla_w_in, 0, n_main), out_dtype=BF16, name="gla_in")
    a_lo = _matmul(h, _Weight(_pad_cols(gla_w_in[0, :, n_main:], LANES)), out_dtype=F32, name="gla_in_lowrank")
    w_up = jnp.pad(gla_w_alpha_up[0], ((0, LANES - gla_w_alpha_up.shape[1]), (0, 0))).astype(BF16)
    y = _gla(proj, a_lo, w_up, gla_b_alpha[0][None], gla_o_norm[0][None], D)
    xs = _matmul_resid(y, _Weight(gla_w_out, 0), xs, g1, name="gla_out")
    xs = _mlp(xs, norm_mlp[0][None], sh2, sc2, g2, w_mlp_up, w_mlp_down, 0)

    kshift, kscale = jnp.split(_ada(c_col, kv_ada_w, kv_ada_b[None], 0), 2, axis=1)
    hk = _norm_mod(xs, kv_norm[None], kshift, kscale)
    kv = _matmul_headnorm(hk, _Weight(kv_w, None, 2 * D), fox_k_norm[None], D, 1.0, name="kv_proj")
    f_logit = _matmul(hk, _Weight(_pad_cols(kv_w[:, 2 * D:], LANES)), out_dtype=F32, name="kv_forget")
    k_aug, v_aug = _fox_kv_augment(f_logit, _pad_cols(fox_b_f[None], LANES), kv, D)

    sh1, sc1, g1, sh2, sc2, g2 = jnp.split(_ada(c_col, ada_w, ada_b[1][None], 1), 6, axis=1)
    h = _norm_mod(xs, norm_mix[1][None], sh1, sc1)
    qg = _matmul_headnorm(h, _Weight(fox_w_in, 0), fox_q_norm[0][None], D, FOX_HEAD ** -0.5 * LOG2E,
                          name="fox_in")
    o = _fox_attention(qg, k_aug, v_aug, D)
    xs = _matmul_resid(o, _Weight(fox_w_out, 0), xs, g1, name="fox_out")
    xs = _mlp(xs, norm_mlp[1][None], sh2, sc2, g2, w_mlp_up, w_mlp_down, 1)
    return xs[None]
```

```python
import functools
from typing import NamedTuple

import numpy as np
import jax
import jax.numpy as jnp
from jax import lax
from jax.experimental import pallas as pl
from jax.experimental.pallas import tpu as pltpu

EPS = 1e-6
LANES = 128
GLA_DK = 256
GLA_DV = 512
GLA_TAU = 16.0
GLA_CHUNK = 128
GLA_HEADS_PER_STEP = 8
FOX_HEAD = 128
FOX_BLOCK = 1024
FOX_SUB = 512
NEG_BIG = -1e30
VMEM_LIMIT_BYTES = 56 * 1024 * 1024
MM_TILE_K = 2048
MM_FULL_K = 4096
MM_VMEM_BUDGET = 44 * 1024 * 1024
MM_MIN_TILE_N = 256
LOG2E = 1.4426950408889634

F32 = jnp.float32
BF16 = jnp.bfloat16


def _params(*sem):
    return pltpu.CompilerParams(dimension_semantics=sem, vmem_limit_bytes=VMEM_LIMIT_BYTES)


def _log_sigmoid(x):
    return jnp.minimum(x, 0.0) - jnp.log1p(jnp.exp(-jnp.abs(x)))


def _silu(x):
    return x * jax.nn.sigmoid(x)


def _ada_body(c_ref, w_ref, b_ref, o_ref, cb_ref, *, tn):
    @pl.when(pl.program_id(0) == 0)
    def _():
        cc = c_ref[...]
        cb_ref[...] = jnp.broadcast_to(_silu(cc), cb_ref.shape)
    cb = cb_ref[...]
    parts = []
    for a in range(tn // LANES):
        parts.append(jnp.sum(w_ref[:, a * LANES:(a + 1) * LANES] * cb, axis=0, keepdims=True))
    o_ref[...] = jnp.concatenate(parts, axis=1) + b_ref[...]


def _ada(c_col, w, b_row, layer):
    D = c_col.shape[0]
    N = w.shape[-1]
    tn = min(512, N)
    if w.ndim == 3:
        w_spec = pl.BlockSpec((None, D, tn), lambda j: (layer, 0, j))
    else:
        w_spec = pl.BlockSpec((D, tn), lambda j: (0, j))
    return pl.pallas_call(
        functools.partial(_ada_body, tn=tn),
        grid=(N // tn,),
        in_specs=[pl.BlockSpec((D, 1), lambda j: (0, 0)), w_spec,
                  pl.BlockSpec((1, tn), lambda j: (0, j))],
        out_specs=pl.BlockSpec((1, tn), lambda j: (0, j)),
        out_shape=jax.ShapeDtypeStruct((1, N), F32),
        scratch_shapes=[pltpu.VMEM((D, LANES), F32)],
        compiler_params=_params("arbitrary"),
        name="ada_gemv",
    )(c_col, w, b_row)


def _normmod_body(x_ref, g_ref, sh_ref, sc_ref, o_ref):
    x = x_ref[...]
    ms = jnp.mean(x * x, axis=-1, keepdims=True)
    y = x * lax.rsqrt(ms + EPS) * g_ref[...]
    o_ref[...] = (y * (1.0 + sc_ref[...]) + sh_ref[...]).astype(o_ref.dtype)


def _norm_mod(x, gain, shift, scale):
    S, D = x.shape
    tr = min(256, S)
    row = pl.BlockSpec((1, D), lambda i: (0, 0))
    return pl.pallas_call(
        _normmod_body,
        grid=(S // tr,),
        in_specs=[pl.BlockSpec((tr, D), lambda i: (i, 0)), row, row, row],
        out_specs=pl.BlockSpec((tr, D), lambda i: (i, 0)),
        out_shape=jax.ShapeDtypeStruct((S, D), BF16),
        compiler_params=_params("parallel"),
        name="norm_mod",
    )(x, gain, shift, scale)


def _ep_cast(acc, extra, o_ref):
    o_ref[...] = acc.astype(o_ref.dtype)


def _ep_sqrelu(acc, extra, o_ref):
    r = jnp.maximum(acc, 0.0)
    o_ref[...] = (r * r).astype(o_ref.dtype)


def _ep_resid(acc, extra, o_ref):
    x_ref, g_ref = extra
    o_ref[...] = x_ref[...] + g_ref[...] * acc


def _ep_headnorm(acc, extra, o_ref, *, n_norm_blocks, scale):
    (gain_ref,) = extra
    j = pl.program_id(1)

    @pl.when(j < n_norm_blocks)
    def _():
        gain = gain_ref[...]
        for g in range(acc.shape[1] // FOX_HEAD):
            blk = acc[:, g * FOX_HEAD:(g + 1) * FOX_HEAD]
            ms = jnp.mean(blk * blk, axis=-1, keepdims=True)
            y = blk * lax.rsqrt(ms + EPS) * gain * scale
            o_ref[:, g * FOX_HEAD:(g + 1) * FOX_HEAD] = y.astype(o_ref.dtype)

    @pl.when(j >= n_norm_blocks)
    def _():
        o_ref[...] = acc.astype(o_ref.dtype)


def _mm_body(*refs, nk, n_extra, epilogue):
    a_ref, b_ref = refs[0], refs[1]
    extra = refs[2:2 + n_extra]
    o_ref = refs[2 + n_extra]
    if nk == 1:
        epilogue(jnp.dot(a_ref[...], b_ref[...].astype(BF16), preferred_element_type=F32), extra, o_ref)
        return
    acc_ref = refs[3 + n_extra]
    k = pl.program_id(2)

    @pl.when(k == 0)
    def _():
        acc_ref[...] = jnp.zeros_like(acc_ref)

    acc_ref[...] += jnp.dot(a_ref[...], b_ref[...].astype(BF16), preferred_element_type=F32)

    @pl.when(k == nk - 1)
    def _():
        epilogue(acc_ref[...], extra, o_ref)


class _Weight(NamedTuple):
    array: jax.Array
    layer: int | None = None

    @property
    def shape(self):
        return self.array.shape[-2:]


def _mm_tiles(M, w, out_dtype, resid=False):
    K, N = w.shape
    tk = K if K <= MM_FULL_K else MM_TILE_K
    nk = K // tk
    out_bytes = jnp.dtype(out_dtype).itemsize
    w_bytes = w.array.dtype.itemsize
    best = None
    for tm in (1024, 512, 256):
        for tn in (1024, 512, 256, LANES):
            if tm > M or M % tm or tn > N or N % tn or (tn < MM_MIN_TILE_N and tn < N):
                continue
            need = 2 * tm * tk * 2 + 2 * tk * tn * w_bytes + 2 * tm * tn * out_bytes
            need += (2 * tm * tn * 4 if resid else 0) + (tm * tn * 4 if nk > 1 else 0)
            if need > MM_VMEM_BUDGET:
                continue
            traffic = M * K * 2 * (N // tn if nk > 1 else 1) + K * N * w_bytes * (M // tm)
            if best is None or (traffic, -tn) < best[0]:
                best = ((traffic, -tn), (tm, tn, tk))
    return best[1]


def _matmul(a, w, *, out_dtype, epilogue=_ep_cast, extras=(), extra_specs=(), tiles=None, name):
    M, K = a.shape
    N = w.shape[1]
    tm, tn, tk = tiles or _mm_tiles(M, w, out_dtype)
    nk = K // tk
    scratch = [pltpu.VMEM((tm, tn), F32)] if nk > 1 else []
    if w.array.ndim == 3:
        w_spec = pl.BlockSpec((None, tk, tn), lambda i, j, k: (w.layer, k, j))
    else:
        w_spec = pl.BlockSpec((tk, tn), lambda i, j, k: (k, j))
    return pl.pallas_call(
        functools.partial(_mm_body, nk=nk, n_extra=len(extras), epilogue=epilogue),
        grid=(M // tm, N // tn, nk),
        in_specs=[pl.BlockSpec((tm, tk), lambda i, j, k: (i, k)), w_spec, *extra_specs],
        out_specs=pl.BlockSpec((tm, tn), lambda i, j, k: (i, j)),
        out_shape=jax.ShapeDtypeStruct((M, N), out_dtype),
        scratch_shapes=scratch,
        compiler_params=_params("parallel", "parallel", "arbitrary"),
        name=name,
    )(a, w.array, *extras)


def _matmul_resid(a, w, x, gate, name):
    M, N = x.shape
    tm, tn, _ = tiles = _mm_tiles(M, w, F32, resid=True)
    return _matmul(a, w, out_dtype=F32, epilogue=_ep_resid, extras=(x, gate),
                   extra_specs=(pl.BlockSpec((tm, tn), lambda i, j, k: (i, j)),
                                pl.BlockSpec((1, tn), lambda i, j, k: (0, j))), tiles=tiles, name=name)


def _matmul_headnorm(a, w, gain, n_norm_cols, scale, name):
    tiles = _mm_tiles(a.shape[0], w, BF16)
    ep = functools.partial(_ep_headnorm, n_norm_blocks=n_norm_cols // tiles[1], scale=scale)
    return _matmul(a, w, out_dtype=BF16, epilogue=ep, extras=(gain,),
                   extra_specs=(pl.BlockSpec((1, FOX_HEAD), lambda i, j, k: (0, 0)),), tiles=tiles, name=name)


def _gla_prefix_matrix(C):
    L = int(np.log2(C))
    t = np.arange(C)[:, None]
    j = np.arange(C)[None, :]
    mats = []
    for l in range(L):
        n = 1 << l
        mid = ((t >> (l + 1)) << (l + 1)) + n
        upper = ((t >> l) & 1) == 1
        mats.append(np.where(upper, (j >= mid) & (j <= t), (j > t) & (j < mid)))
    mats.append(j <= t)
    mats.append(j > t)
    return np.concatenate(mats, axis=0).astype(np.float32)


def _gla_body(q_ref, k_ref, v_ref, r_ref, alo_ref, wup_ref, balpha_ref, onorm_ref, f_ref,
              o_ref, state_ref, *, C, L, HP):
    c = pl.program_id(0)
    hb = pl.program_id(1)
    row = lax.broadcasted_iota(jnp.int32, (C, C), 0)
    col = lax.broadcasted_iota(jnp.int32, (C, C), 1)
    a_lo = alo_ref[...].astype(BF16)
    fmat = f_ref[...]

    heads = range(HP)
    kcols = [slice(j * GLA_DK, (j + 1) * GLA_DK) for j in heads]
    vcols = [slice(j * GLA_DV, (j + 1) * GLA_DV) for j in heads]

    @pl.when(c == 0)
    def _():
        for j in heads:
            state_ref[hb * HP + j] = jnp.zeros(state_ref.shape[1:], F32)

    st = [state_ref[hb * HP + j] for j in heads]
    q = [q_ref[:, kcols[j]].astype(F32) * (GLA_DK ** -0.5) for j in heads]
    k = [k_ref[:, kcols[j]].astype(F32) for j in heads]
    v = [v_ref[:, vcols[j]] for j in heads]

    xs = []
    for j in heads:
        z = jnp.dot(a_lo, wup_ref[:, kcols[j]], preferred_element_type=F32) + balpha_ref[:, kcols[j]]
        la = _log_sigmoid(z) * (1.0 / GLA_TAU)
        hi = la.astype(BF16)
        mid = (la - hi.astype(F32)).astype(BF16)
        x = jnp.dot(fmat, jnp.concatenate([hi, mid], axis=1), preferred_element_type=F32)
        xs.append(x[:, :GLA_DK] + x[:, GLA_DK:])

    scores = [jnp.where(row == col, jnp.sum(q[j] * k[j], axis=-1, keepdims=True), 0.0) for j in heads]
    for l in range(L):
        rl = row >> l
        cl = col >> l
        pick = ((rl ^ cl) == 1) & (rl > cl)
        for j in heads:
            e = jnp.exp(xs[j][l * C:(l + 1) * C])
            s_l = lax.dot_general((q[j] * e).astype(BF16), (k[j] * e).astype(BF16),
                                  (((1,), (1,)), ((), ())), preferred_element_type=F32)
            scores[j] = jnp.where(pick, s_l, scores[j])

    outs = []
    for j in heads:
        b = xs[j][L * C:(L + 1) * C]
        rem = xs[j][(L + 1) * C:]
        o = jnp.dot(scores[j].astype(BF16), v[j], preferred_element_type=F32)
        o += jnp.dot((q[j] * jnp.exp(b)).astype(BF16), st[j].astype(BF16), preferred_element_type=F32)
        outs.append(o)
        kt = jnp.transpose(k[j] * jnp.exp(rem)).astype(BF16)
        d_row = jnp.exp(b[C - 1:C, :])
        d_col = jnp.transpose(jnp.broadcast_to(d_row, (LANES, GLA_DK)))[:, :1]
        st[j] = d_col * st[j] + jnp.dot(kt, v[j], preferred_element_type=F32)

    for j in heads:
        state_ref[hb * HP + j] = st[j]
        o = outs[j]
        ms = jnp.mean(o * o, axis=-1, keepdims=True)
        y = o * lax.rsqrt(ms + EPS) * onorm_ref[...]
        o_ref[:, vcols[j]] = (y * _silu(r_ref[:, vcols[j]].astype(F32))).astype(o_ref.dtype)


def _gla(proj, a_lo, w_up, b_alpha, o_norm, D):
    S = proj.shape[0]
    H = D // GLA_DV
    KW = H * GLA_DK
    C = min(GLA_CHUNK, S)
    L = int(np.log2(C))
    HP = min(GLA_HEADS_PER_STEP, H)
    fmat = jnp.asarray(_gla_prefix_matrix(C), BF16)
    wk, wv = HP * GLA_DK, HP * GLA_DV
    kq, kv, kr = KW // wk, (2 * KW) // wv, (2 * KW + D) // wv
    return pl.pallas_call(
        functools.partial(_gla_body, C=C, L=L, HP=HP),
        grid=(S // C, H // HP),
        in_specs=[
            pl.BlockSpec((C, wk), lambda c, h: (c, h)),
            pl.BlockSpec((C, wk), lambda c, h: (c, kq + h)),
            pl.BlockSpec((C, wv), lambda c, h: (c, kv + h)),
            pl.BlockSpec((C, wv), lambda c, h: (c, kr + h)),
            pl.BlockSpec((C, LANES), lambda c, h: (c, 0)),
            pl.BlockSpec((LANES, wk), lambda c, h: (0, h)),
            pl.BlockSpec((1, wk), lambda c, h: (0, h)),
            pl.BlockSpec((1, GLA_DV), lambda c, h: (0, 0)),
            pl.BlockSpec(fmat.shape, lambda c, h: (0, 0)),
        ],
        out_specs=pl.BlockSpec((C, wv), lambda c, h: (c, h)),
        out_shape=jax.ShapeDtypeStruct((S, D), BF16),
        scratch_shapes=[pltpu.VMEM((H, GLA_DK, GLA_DV), F32)],
        compiler_params=_params("arbitrary", "arbitrary"),
        name="gla_chunk",
    )(proj, proj, proj, proj, a_lo, w_up, b_alpha, o_norm, fmat)


def _split3(x):
    hi = x.astype(BF16)
    r1 = x - hi.astype(F32)
    mid = r1.astype(BF16)
    lo = (r1 - mid.astype(F32)).astype(BF16)
    return hi, mid, lo


def _kvaug_body(f_ref, bf_ref, tri_ref, k_ref, v_ref, ka_ref, va_ref, carry_ref, *, T, H):
    @pl.when(pl.program_id(0) == 0)
    def _():
        carry_ref[...] = jnp.zeros_like(carry_ref)

    lf = _log_sigmoid(f_ref[...] + bf_ref[...])
    cs = jnp.dot(tri_ref[...], jnp.concatenate(_split3(lf), axis=1), preferred_element_type=F32)
    cum = cs[:, :LANES] + cs[:, LANES:2 * LANES] + cs[:, 2 * LANES:] + carry_ref[...]
    carry_ref[...] = cum[T - 1:T, :]
    lane = lax.broadcasted_iota(jnp.int32, (T, LANES), 1)
    v_tail = jnp.where(lane == 0, 1.0, 0.0).astype(BF16)
    for h in range(H):
        val = cum[:, h:h + 1] * (-LOG2E)
        hi = val.astype(BF16).astype(F32)
        mid = (val - hi).astype(BF16).astype(F32)
        lo = val - hi - mid
        k_tail = jnp.where(lane == 0, hi, jnp.where(lane == 1, mid, jnp.where(lane == 2, lo, 0.0)))
        head = slice(h * FOX_HEAD, (h + 1) * FOX_HEAD)
        ka_ref[:, 2 * h * LANES:(2 * h + 1) * LANES] = k_ref[:, head]
        ka_ref[:, (2 * h + 1) * LANES:(2 * h + 2) * LANES] = k_tail.astype(BF16)
        va_ref[:, 2 * h * LANES:(2 * h + 1) * LANES] = v_ref[:, head]
        va_ref[:, (2 * h + 1) * LANES:(2 * h + 2) * LANES] = v_tail


def _fox_kv_augment(f_logit, b_f, kv, D):
    S = f_logit.shape[0]
    H = D // FOX_HEAD
    T = min(256, S)
    tri = jnp.asarray(np.tril(np.ones((T, T), np.float32)), BF16)
    aug = jax.ShapeDtypeStruct((S, 2 * D), BF16)
    return pl.pallas_call(
        functools.partial(_kvaug_body, T=T, H=H),
        grid=(S // T,),
        in_specs=[pl.BlockSpec((T, LANES), lambda i: (i, 0)),
                  pl.BlockSpec((1, LANES), lambda i: (0, 0)),
                  pl.BlockSpec((T, T), lambda i: (0, 0)),
                  pl.BlockSpec((T, D), lambda i: (i, 0)),
                  pl.BlockSpec((T, D), lambda i: (i, 1))],
        out_specs=[pl.BlockSpec((T, 2 * D), lambda i: (i, 0)), pl.BlockSpec((T, 2 * D), lambda i: (i, 0))],
        out_shape=[aug, aug],
        scratch_shapes=[pltpu.VMEM((1, LANES), F32)],
        compiler_params=_params("arbitrary"),
        name="fox_kv_augment",
    )(f_logit, b_f, tri, kv, kv)


def _fox_body(q_ref, k_ref, v_ref, og_ref, o_ref, qa_ref, sa_ref, sb_ref, xa_ref, xb_ref, m_ref, acc_ref,
              *, tq, tk, tr):
    i = pl.program_id(1)
    nsub = tq // tr
    lane_q = lax.broadcasted_iota(jnp.int32, (tq, LANES), 1)
    qa_ref[:, :FOX_HEAD] = q_ref[...]
    qa_ref[:, FOX_HEAD:] = jnp.where(lane_q < 3, 1.0, 0.0).astype(BF16)
    m_ref[...] = jnp.full(m_ref.shape, NEG_BIG, F32)
    acc_ref[...] = jnp.zeros(acc_ref.shape, F32)

    def logits_to(cidx, s_ref, x_ref):
        kc = k_ref[pl.ds(pl.multiple_of(cidx * tk, tk), tk), :]
        for r in range(nsub):
            rows = slice(r * tr, (r + 1) * tr)
            s = lax.dot_general(qa_ref[rows, :], kc, (((1,), (1,)), ((), ())), preferred_element_type=F32)
            s_ref[rows, :] = s
            x_ref[rows, :] = jnp.max(s, axis=-1, keepdims=True)

    def softmax_pv(cidx, s_ref, x_ref, masked):
        c0 = pl.multiple_of(cidx * tk, tk)
        for r in range(nsub):
            rows = slice(r * tr, (r + 1) * tr)
            if masked:
                ncol = (r + 1) * tr
                row = lax.broadcasted_iota(jnp.int32, (tr, ncol), 0) + r * tr
                col = lax.broadcasted_iota(jnp.int32, (tr, ncol), 1)
                s = jnp.where(row >= col, s_ref[rows, :ncol], NEG_BIG)
                s_max = jnp.max(s, axis=-1, keepdims=True)
            else:
                ncol = tk
                s = s_ref[rows, :]
                s_max = x_ref[rows, :]
            vc = v_ref[pl.ds(c0, ncol), :]
            m_old = m_ref[rows, :]
            m_new = jnp.maximum(m_old, s_max)
            m_ref[rows, :] = m_new
            p = jnp.exp2(s - m_new).astype(BF16)
            acc_ref[rows, :] = (jnp.exp2(m_old - m_new) * acc_ref[rows, :]
                                + jnp.dot(p, vc, preferred_element_type=F32))

    def finalize():
        for r in range(nsub):
            rows = slice(r * tr, (r + 1) * tr)
            gate = jax.nn.sigmoid(og_ref[rows, :].astype(F32))
            inv_l = 1.0 / acc_ref[rows, FOX_HEAD:FOX_HEAD + 1]
            o_ref[rows, :] = (acc_ref[rows, :FOX_HEAD] * inv_l * gate).astype(o_ref.dtype)

    def pair(j, carry):
        c = 2 * j
        logits_to(c + 1, sb_ref, xb_ref)
        softmax_pv(c, sa_ref, xa_ref, False)
        logits_to(c + 2, sa_ref, xa_ref)
        softmax_pv(c + 1, sb_ref, xb_ref, False)
        return carry

    logits_to(0, sa_ref, xa_ref)
    lax.fori_loop(0, i // 2, pair, 0)

    @pl.when(i % 2 == 0)
    def _():
        softmax_pv(i, sa_ref, xa_ref, True)
        finalize()

    @pl.when(i % 2 == 1)
    def _():
        logits_to(i, sb_ref, xb_ref)
        softmax_pv(i - 1, sa_ref, xa_ref, False)
        softmax_pv(i, sb_ref, xb_ref, True)
        finalize()


def _fox_attention(qg, k_aug, v_aug, D):
    S = qg.shape[0]
    H = D // FOX_HEAD
    tq = tk = min(FOX_BLOCK, S)
    return pl.pallas_call(
        functools.partial(_fox_body, tq=tq, tk=tk, tr=min(FOX_SUB, tq)),
        grid=(H, S // tq),
        in_specs=[
            pl.BlockSpec((tq, FOX_HEAD), lambda h, i: (i, h)),
            pl.BlockSpec((S, 2 * LANES), lambda h, i: (0, h)),
            pl.BlockSpec((S, 2 * LANES), lambda h, i: (0, h)),
            pl.BlockSpec((tq, FOX_HEAD), lambda h, i: (i, H + h)),
        ],
        out_specs=pl.BlockSpec((tq, FOX_HEAD), lambda h, i: (i, h)),
        out_shape=jax.ShapeDtypeStruct((S, D), BF16),
        scratch_shapes=[pltpu.VMEM((tq, 2 * LANES), BF16), pltpu.VMEM((tq, tk), F32), pltpu.VMEM((tq, tk), F32),
                        pltpu.VMEM((tq, 1), F32), pltpu.VMEM((tq, 1), F32),
                        pltpu.VMEM((tq, 1), F32), pltpu.VMEM((tq, 2 * FOX_HEAD), F32)],
        compiler_params=_params("arbitrary", "arbitrary"),
        name="fox_attention",
    )(qg, k_aug, v_aug, qg)


def _pad_cols(w, n):
    return jnp.pad(w, ((0, 0), (0, n - w.shape[1])))


def _mlp(x, gain, shift, scale, gate, w_up, w_down, layer):
    h = _norm_mod(x, gain, shift, scale)
    u = _matmul(h, _Weight(w_up, layer), out_dtype=BF16, epilogue=_ep_sqrelu, name=f"mlp_up_l{layer}")
    return _matmul_resid(u, _Weight(w_down[layer].astype(BF16)), x, gate, name=f"mlp_down_l{layer}")


def kernel(x, c, ada_w, ada_b, norm_mix, norm_mlp, w_mlp_up, w_mlp_down, gla_w_in, gla_w_alpha_up,
           gla_b_alpha, gla_o_norm, gla_w_out, kv_ada_w, kv_ada_b, kv_norm, kv_w, fox_b_f, fox_k_norm,
           fox_w_in, fox_q_norm, fox_w_out):
    B, S, D = x.shape
    assert B == 1 and ada_w.shape[0] == 2
    xs = x[0]
    c_col = c.reshape(D, 1)
    KW = (D // GLA_DV) * GLA_DK

    sh1, sc1, g1, sh2, sc2, g2 = jnp.split(_ada(c_col, ada_w, ada_b[0][None], 0), 6, axis=1)
    h = _norm_mod(xs, norm_mix[0][None], sh1, sc1)
    n_main = 2 * KW + 2 * D
    proj = _matmul(h, _Weight(gla_w_in[0, :, :n_main].astype(BF16)), out_dtype=BF16, name="gla_in")
    a_lo = _matmul(h, _Weight(_pad_cols(gla_w_in[0, :, n_main:], LANES)), out_dtype=F32, name="gla_in_lowrank")
    w_up = jnp.pad(gla_w_alpha_up[0], ((0, LANES - gla_w_alpha_up.shape[1]), (0, 0))).astype(BF16)
    y = _gla(proj, a_lo, w_up, gla_b_alpha[0][None], gla_o_norm[0][None], D)
    xs = _matmul_resid(y, _Weight(gla_w_out, 0), xs, g1, name="gla_out")
    xs = _mlp(xs, norm_mlp[0][None], sh2, sc2, g2, w_mlp_up, w_mlp_down, 0)

    kshift, kscale = jnp.split(_ada(c_col, kv_ada_w, kv_ada_b[None], 0), 2, axis=1)
    hk = _norm_mod(xs, kv_norm[None], kshift, kscale)
    kv = _matmul_headnorm(hk, _Weight(kv_w[:, :2 * D].astype(BF16)), fox_k_norm[None], D, 1.0, name="kv_proj")
    f_logit = _matmul(hk, _Weight(_pad_cols(kv_w[:, 2 * D:], LANES)), out_dtype=F32, name="kv_forget")
    k_aug, v_aug = _fox_kv_augment(f_logit, _pad_cols(fox_b_f[None], LANES), kv, D)

    sh1, sc1, g1, sh2, sc2, g2 = jnp.split(_ada(c_col, ada_w, ada_b[1][None], 1), 6, axis=1)
    h = _norm_mod(xs, norm_mix[1][None], sh1, sc1)
    qg = _matmul_headnorm(h, _Weight(fox_w_in, 0), fox_q_norm[0][None], D, FOX_HEAD ** -0.5 * LOG2E,
                          name="fox_in")
    o = _fox_attention(qg, k_aug, v_aug, D)
    xs = _matmul_resid(o, _Weight(fox_w_out, 0), xs, g1, name="fox_out")
    xs = _mlp(xs, norm_mlp[1][None], sh2, sc2, g2, w_mlp_up, w_mlp_down, 1)
    return xs[None]
```

```python
import functools
from typing import NamedTuple

import numpy as np
import jax
import jax.numpy as jnp
from jax import lax
from jax.experimental import pallas as pl
from jax.experimental.pallas import tpu as pltpu

EPS = 1e-6
LANES = 128
GLA_DK = 256
GLA_DV = 512
GLA_TAU = 16.0
GLA_CHUNK = 128
GLA_HEADS_PER_STEP = 8
FOX_HEAD = 128
FOX_BLOCK = 1024
FOX_SUB = 512
NEG_BIG = -1e30
VMEM_LIMIT_BYTES = 56 * 1024 * 1024
MM_TILE_K = 2048
MM_FULL_K = 4096
MM_VMEM_BUDGET = 44 * 1024 * 1024
MM_MIN_TILE_N = 256
LOG2E = 1.4426950408889634

F32 = jnp.float32
BF16 = jnp.bfloat16


def _params(*sem):
    return pltpu.CompilerParams(dimension_semantics=sem, vmem_limit_bytes=VMEM_LIMIT_BYTES)


def _log_sigmoid(x):
    return jnp.minimum(x, 0.0) - jnp.log1p(jnp.exp(-jnp.abs(x)))


def _silu(x):
    return x * jax.nn.sigmoid(x)


def _ada_body(c_ref, w_ref, b_ref, o_ref, cb_ref, *, tn):
    @pl.when(pl.program_id(0) == 0)
    def _():
        cc = c_ref[...]
        cb_ref[...] = jnp.broadcast_to(_silu(cc), cb_ref.shape)
    cb = cb_ref[...]
    parts = []
    for a in range(tn // LANES):
        parts.append(jnp.sum(w_ref[:, a * LANES:(a + 1) * LANES] * cb, axis=0, keepdims=True))
    o_ref[...] = jnp.concatenate(parts, axis=1) + b_ref[...]


def _ada(c_col, w, b_row, layer):
    D = c_col.shape[0]
    N = w.shape[-1]
    tn = min(512, N)
    if w.ndim == 3:
        w_spec = pl.BlockSpec((None, D, tn), lambda j: (layer, 0, j))
    else:
        w_spec = pl.BlockSpec((D, tn), lambda j: (0, j))
    return pl.pallas_call(
        functools.partial(_ada_body, tn=tn),
        grid=(N // tn,),
        in_specs=[pl.BlockSpec((D, 1), lambda j: (0, 0)), w_spec,
                  pl.BlockSpec((1, tn), lambda j: (0, j))],
        out_specs=pl.BlockSpec((1, tn), lambda j: (0, j)),
        out_shape=jax.ShapeDtypeStruct((1, N), F32),
        scratch_shapes=[pltpu.VMEM((D, LANES), F32)],
        compiler_params=_params("arbitrary"),
        name="ada_gemv",
    )(c_col, w, b_row)


def _normmod_body(x_ref, g_ref, sh_ref, sc_ref, o_ref):
    x = x_ref[...]
    ms = jnp.mean(x * x, axis=-1, keepdims=True)
    y = x * lax.rsqrt(ms + EPS) * g_ref[...]
    o_ref[...] = (y * (1.0 + sc_ref[...]) + sh_ref[...]).astype(o_ref.dtype)


def _norm_mod(x, gain, shift, scale):
    S, D = x.shape
    tr = min(512, S)
    row = pl.BlockSpec((1, D), lambda i: (0, 0))
    return pl.pallas_call(
        _normmod_body,
        grid=(S // tr,),
        in_specs=[pl.BlockSpec((tr, D), lambda i: (i, 0)), row, row, row],
        out_specs=pl.BlockSpec((tr, D), lambda i: (i, 0)),
        out_shape=jax.ShapeDtypeStruct((S, D), BF16),
        compiler_params=_params("parallel"),
        name="norm_mod",
    )(x, gain, shift, scale)


def _cast_body(x_ref, o_ref):
    o_ref[...] = x_ref[...].astype(o_ref.dtype)


def _cast_bf16(w):
    L, K, N = w.shape
    tr = min(512, K)
    spec = pl.BlockSpec((None, tr, N), lambda l, i: (l, i, 0))
    return pl.pallas_call(
        _cast_body,
        grid=(L, K // tr),
        in_specs=[spec],
        out_specs=spec,
        out_shape=jax.ShapeDtypeStruct(w.shape, BF16),
        compiler_params=_params("parallel", "parallel"),
        name="cast_bf16",
    )(w)


def _ep_cast(acc, extra, o_ref):
    o_ref[...] = acc.astype(o_ref.dtype)


def _ep_sqrelu(acc, extra, o_ref):
    r = jnp.maximum(acc, 0.0)
    o_ref[...] = (r * r).astype(o_ref.dtype)


def _ep_resid(acc, extra, o_ref):
    x_ref, g_ref = extra
    o_ref[...] = x_ref[...] + g_ref[...] * acc


def _ep_headnorm(acc, extra, o_ref, *, n_norm_blocks, scale):
    (gain_ref,) = extra
    j = pl.program_id(1)

    @pl.when(j < n_norm_blocks)
    def _():
        gain = gain_ref[...]
        for g in range(acc.shape[1] // FOX_HEAD):
            blk = acc[:, g * FOX_HEAD:(g + 1) * FOX_HEAD]
            ms = jnp.mean(blk * blk, axis=-1, keepdims=True)
            y = blk * lax.rsqrt(ms + EPS) * gain * scale
            o_ref[:, g * FOX_HEAD:(g + 1) * FOX_HEAD] = y.astype(o_ref.dtype)

    @pl.when(j >= n_norm_blocks)
    def _():
        o_ref[...] = acc.astype(o_ref.dtype)


def _mm_body(*refs, nk, n_extra, epilogue):
    a_ref, b_ref = refs[0], refs[1]
    extra = refs[2:2 + n_extra]
    o_ref = refs[2 + n_extra]
    if nk == 1:
        epilogue(jnp.dot(a_ref[...], b_ref[...].astype(BF16), preferred_element_type=F32), extra, o_ref)
        return
    acc_ref = refs[3 + n_extra]
    k = pl.program_id(2)

    @pl.when(k == 0)
    def _():
        acc_ref[...] = jnp.zeros_like(acc_ref)

    acc_ref[...] += jnp.dot(a_ref[...], b_ref[...].astype(BF16), preferred_element_type=F32)

    @pl.when(k == nk - 1)
    def _():
        epilogue(acc_ref[...], extra, o_ref)


class _Weight(NamedTuple):
    array: jax.Array
    layer: int | None = None

    @property
    def shape(self):
        return self.array.shape[-2:]


def _mm_tiles(M, w, out_dtype, resid=False):
    K, N = w.shape
    tk = K if K <= MM_FULL_K else MM_TILE_K
    nk = K // tk
    out_bytes = jnp.dtype(out_dtype).itemsize
    w_bytes = w.array.dtype.itemsize
    best = None
    for tm in (1024, 512, 256):
        for tn in (1024, 512, 256, LANES):
            if tm > M or M % tm or tn > N or N % tn or (tn < MM_MIN_TILE_N and tn < N):
                continue
            need = 2 * tm * tk * 2 + 2 * tk * tn * w_bytes + 2 * tm * tn * out_bytes
            need += (2 * tm * tn * 4 if resid else 0) + (tm * tn * 4 if nk > 1 else 0)
            if need > MM_VMEM_BUDGET:
                continue
            traffic = M * K * 2 * (N // tn if nk > 1 else 1) + K * N * w_bytes * (M // tm)
            if best is None or (traffic, -tn) < best[0]:
                best = ((traffic, -tn), (tm, tn, tk))
    return best[1]


def _matmul(a, w, *, out_dtype, epilogue=_ep_cast, extras=(), extra_specs=(), tiles=None, name):
    M, K = a.shape
    N = w.shape[1]
    tm, tn, tk = tiles or _mm_tiles(M, w, out_dtype)
    nk = K // tk
    scratch = [pltpu.VMEM((tm, tn), F32)] if nk > 1 else []
    if w.array.ndim == 3:
        w_spec = pl.BlockSpec((None, tk, tn), lambda i, j, k: (w.layer, k, j))
    else:
        w_spec = pl.BlockSpec((tk, tn), lambda i, j, k: (k, j))
    return pl.pallas_call(
        functools.partial(_mm_body, nk=nk, n_extra=len(extras), epilogue=epilogue),
        grid=(M // tm, N // tn, nk),
        in_specs=[pl.BlockSpec((tm, tk), lambda i, j, k: (i, k)), w_spec, *extra_specs],
        out_specs=pl.BlockSpec((tm, tn), lambda i, j, k: (i, j)),
        out_shape=jax.ShapeDtypeStruct((M, N), out_dtype),
        scratch_shapes=scratch,
        compiler_params=_params("parallel", "parallel", "arbitrary"),
        name=name,
    )(a, w.array, *extras)


def _matmul_resid(a, w, x, gate, name):
    M, N = x.shape
    tm, tn, _ = tiles = _mm_tiles(M, w, F32, resid=True)
    return _matmul(a, w, out_dtype=F32, epilogue=_ep_resid, extras=(x, gate),
                   extra_specs=(pl.BlockSpec((tm, tn), lambda i, j, k: (i, j)),
                                pl.BlockSpec((1, tn), lambda i, j, k: (0, j))), tiles=tiles, name=name)


def _matmul_headnorm(a, w, gain, n_norm_cols, scale, name):
    tiles = _mm_tiles(a.shape[0], w, BF16)
    ep = functools.partial(_ep_headnorm, n_norm_blocks=n_norm_cols // tiles[1], scale=scale)
    return _matmul(a, w, out_dtype=BF16, epilogue=ep, extras=(gain,),
                   extra_specs=(pl.BlockSpec((1, FOX_HEAD), lambda i, j, k: (0, 0)),), tiles=tiles, name=name)


def _gla_prefix_matrix(C):
    L = int(np.log2(C))
    t = np.arange(C)[:, None]
    j = np.arange(C)[None, :]
    mats = []
    for l in range(L):
        n = 1 << l
        mid = ((t >> (l + 1)) << (l + 1)) + n
        upper = ((t >> l) & 1) == 1
        mats.append(np.where(upper, (j >= mid) & (j <= t), (j > t) & (j < mid)))
    mats.append(j <= t)
    mats.append(j > t)
    return np.concatenate(mats, axis=0).astype(np.float32)


def _gla_body(q_ref, k_ref, v_ref, r_ref, alo_ref, wup_ref, balpha_ref, onorm_ref, f_ref,
              o_ref, state_ref, *, C, L, HP):
    c = pl.program_id(0)
    hb = pl.program_id(1)
    row = lax.broadcasted_iota(jnp.int32, (C, C), 0)
    col = lax.broadcasted_iota(jnp.int32, (C, C), 1)
    a_lo = alo_ref[...].astype(BF16)
    fmat = f_ref[...]

    heads = range(HP)
    kcols = [slice(j * GLA_DK, (j + 1) * GLA_DK) for j in heads]
    vcols = [slice(j * GLA_DV, (j + 1) * GLA_DV) for j in heads]

    @pl.when(c == 0)
    def _():
        for j in heads:
            state_ref[hb * HP + j] = jnp.zeros(state_ref.shape[1:], F32)

    st = [state_ref[hb * HP + j] for j in heads]
    q = [q_ref[:, kcols[j]].astype(F32) * (GLA_DK ** -0.5) for j in heads]
    k = [k_ref[:, kcols[j]].astype(F32) for j in heads]
    v = [v_ref[:, vcols[j]] for j in heads]

    xs = []
    for j in heads:
        z = jnp.dot(a_lo, wup_ref[:, kcols[j]], preferred_element_type=F32) + balpha_ref[:, kcols[j]]
        la = _log_sigmoid(z) * (1.0 / GLA_TAU)
        hi = la.astype(BF16)
        mid = (la - hi.astype(F32)).astype(BF16)
        x = jnp.dot(fmat, jnp.concatenate([hi, mid], axis=1), preferred_element_type=F32)
        xs.append(x[:, :GLA_DK] + x[:, GLA_DK:])

    scores = [jnp.where(row == col, jnp.sum(q[j] * k[j], axis=-1, keepdims=True), 0.0) for j in heads]
    for l in range(L):
        rl = row >> l
        cl = col >> l
        pick = ((rl ^ cl) == 1) & (rl > cl)
        for j in heads:
            e = jnp.exp(xs[j][l * C:(l + 1) * C])
            s_l = lax.dot_general((q[j] * e).astype(BF16), (k[j] * e).astype(BF16),
                                  (((1,), (1,)), ((), ())), preferred_element_type=F32)
            scores[j] = jnp.where(pick, s_l, scores[j])

    outs = []
    for j in heads:
        b = xs[j][L * C:(L + 1) * C]
        rem = xs[j][(L + 1) * C:]
        o = jnp.dot(scores[j].astype(BF16), v[j], preferred_element_type=F32)
        o += jnp.dot((q[j] * jnp.exp(b)).astype(BF16), st[j].astype(BF16), preferred_element_type=F32)
        outs.append(o)
        kt = jnp.transpose(k[j] * jnp.exp(rem)).astype(BF16)
        d_row = jnp.exp(b[C - 1:C, :])
        d_col = jnp.transpose(jnp.broadcast_to(d_row, (LANES, GLA_DK)))[:, :1]
        st[j] = d_col * st[j] + jnp.dot(kt, v[j], preferred_element_type=F32)

    for j in heads:
        state_ref[hb * HP + j] = st[j]
        o = outs[j]
        ms = jnp.mean(o * o, axis=-1, keepdims=True)
        y = o * lax.rsqrt(ms + EPS) * onorm_ref[...]
        o_ref[:, vcols[j]] = (y * _silu(r_ref[:, vcols[j]].astype(F32))).astype(o_ref.dtype)


def _gla(proj, a_lo, w_up, b_alpha, o_norm, D):
    S = proj.shape[0]
    H = D // GLA_DV
    KW = H * GLA_DK
    C = min(GLA_CHUNK, S)
    L = int(np.log2(C))
    HP = min(GLA_HEADS_PER_STEP, H)
    fmat = jnp.asarray(_gla_prefix_matrix(C), BF16)
    wk, wv = HP * GLA_DK, HP * GLA_DV
    kq, kv, kr = KW // wk, (2 * KW) // wv, (2 * KW + D) // wv
    return pl.pallas_call(
        functools.partial(_gla_body, C=C, L=L, HP=HP),
        grid=(S // C, H // HP),
        in_specs=[
            pl.BlockSpec((C, wk), lambda c, h: (c, h)),
            pl.BlockSpec((C, wk), lambda c, h: (c, kq + h)),
            pl.BlockSpec((C, wv), lambda c, h: (c, kv + h)),
            pl.BlockSpec((C, wv), lambda c, h: (c, kr + h)),
            pl.BlockSpec((C, LANES), lambda c, h: (c, 0)),
            pl.BlockSpec((LANES, wk), lambda c, h: (0, h)),
            pl.BlockSpec((1, wk), lambda c, h: (0, h)),
            pl.BlockSpec((1, GLA_DV), lambda c, h: (0, 0)),
            pl.BlockSpec(fmat.shape, lambda c, h: (0, 0)),
        ],
        out_specs=pl.BlockSpec((C, wv), lambda c, h: (c, h)),
        out_shape=jax.ShapeDtypeStruct((S, D), BF16),
        scratch_shapes=[pltpu.VMEM((H, GLA_DK, GLA_DV), F32)],
        compiler_params=_params("arbitrary", "arbitrary"),
        name="gla_chunk",
    )(proj, proj, proj, proj, a_lo, w_up, b_alpha, o_norm, fmat)


def _split3(x):
    hi = x.astype(BF16)
    r1 = x - hi.astype(F32)
    mid = r1.astype(BF16)
    lo = (r1 - mid.astype(F32)).astype(BF16)
    return hi, mid, lo


def _kvaug_body(f_ref, bf_ref, tri_ref, kn_ref, k_ref, v_ref, ka_ref, va_ref, carry_ref, *, T, H):
    @pl.when(pl.program_id(0) == 0)
    def _():
        carry_ref[...] = jnp.zeros_like(carry_ref)

    lf = _log_sigmoid(f_ref[...] + bf_ref[...])
    cs = jnp.dot(tri_ref[...], jnp.concatenate(_split3(lf), axis=1), preferred_element_type=F32)
    cum = cs[:, :LANES] + cs[:, LANES:2 * LANES] + cs[:, 2 * LANES:] + carry_ref[...]
    carry_ref[...] = cum[T - 1:T, :]
    lane = lax.broadcasted_iota(jnp.int32, (T, LANES), 1)
    v_tail = jnp.where(lane == 0, 1.0, 0.0).astype(BF16)
    for h in range(H):
        val = cum[:, h:h + 1] * (-LOG2E)
        hi = val.astype(BF16).astype(F32)
        mid = (val - hi).astype(BF16).astype(F32)
        lo = val - hi - mid
        k_tail = jnp.where(lane == 0, hi, jnp.where(lane == 1, mid, jnp.where(lane == 2, lo, 0.0)))
        head = slice(h * FOX_HEAD, (h + 1) * FOX_HEAD)
        kh = k_ref[:, head].astype(F32)
        kh = kh * lax.rsqrt(jnp.mean(kh * kh, axis=-1, keepdims=True) + EPS) * kn_ref[...]
        ka_ref[:, 2 * h * LANES:(2 * h + 1) * LANES] = kh.astype(BF16)
        ka_ref[:, (2 * h + 1) * LANES:(2 * h + 2) * LANES] = k_tail.astype(BF16)
        va_ref[:, 2 * h * LANES:(2 * h + 1) * LANES] = v_ref[:, head]
        va_ref[:, (2 * h + 1) * LANES:(2 * h + 2) * LANES] = v_tail


def _fox_kv_augment(f_logit, b_f, k_gain, kv, D):
    S = f_logit.shape[0]
    H = D // FOX_HEAD
    T = min(256, S)
    tri = jnp.asarray(np.tril(np.ones((T, T), np.float32)), BF16)
    aug = jax.ShapeDtypeStruct((S, 2 * D), BF16)
    return pl.pallas_call(
        functools.partial(_kvaug_body, T=T, H=H),
        grid=(S // T,),
        in_specs=[pl.BlockSpec((T, LANES), lambda i: (i, 0)),
                  pl.BlockSpec((1, LANES), lambda i: (0, 0)),
                  pl.BlockSpec((T, T), lambda i: (0, 0)),
                  pl.BlockSpec((1, FOX_HEAD), lambda i: (0, 0)),
                  pl.BlockSpec((T, D), lambda i: (i, 0)),
                  pl.BlockSpec((T, D), lambda i: (i, 1))],
        out_specs=[pl.BlockSpec((T, 2 * D), lambda i: (i, 0)), pl.BlockSpec((T, 2 * D), lambda i: (i, 0))],
        out_shape=[aug, aug],
        scratch_shapes=[pltpu.VMEM((1, LANES), F32)],
        compiler_params=_params("arbitrary"),
        name="fox_kv_augment",
    )(f_logit, b_f, tri, k_gain, kv, kv)


def _fox_body(q_ref, k_ref, v_ref, og_ref, o_ref, qa_ref, sa_ref, sb_ref, xa_ref, xb_ref, m_ref, acc_ref,
              *, tq, tk, tr):
    i = pl.program_id(1)
    nsub = tq // tr
    lane_q = lax.broadcasted_iota(jnp.int32, (tq, LANES), 1)
    qa_ref[:, :FOX_HEAD] = q_ref[...]
    qa_ref[:, FOX_HEAD:] = jnp.where(lane_q < 3, 1.0, 0.0).astype(BF16)
    m_ref[...] = jnp.full(m_ref.shape, NEG_BIG, F32)
    acc_ref[...] = jnp.zeros(acc_ref.shape, F32)

    def logits_to(cidx, s_ref, x_ref):
        kc = k_ref[pl.ds(pl.multiple_of(cidx * tk, tk), tk), :]
        for r in range(nsub):
            rows = slice(r * tr, (r + 1) * tr)
            s = lax.dot_general(qa_ref[rows, :], kc, (((1,), (1,)), ((), ())), preferred_element_type=F32)
            s_ref[rows, :] = s
            x_ref[rows, :] = jnp.max(s, axis=-1, keepdims=True)

    def softmax_pv(cidx, s_ref, x_ref, masked):
        c0 = pl.multiple_of(cidx * tk, tk)
        for r in range(nsub):
            rows = slice(r * tr, (r + 1) * tr)
            if masked:
                ncol = (r + 1) * tr
                row = lax.broadcasted_iota(jnp.int32, (tr, ncol), 0) + r * tr
                col = lax.broadcasted_iota(jnp.int32, (tr, ncol), 1)
                s = jnp.where(row >= col, s_ref[rows, :ncol], NEG_BIG)
                s_max = jnp.max(s, axis=-1, keepdims=True)
            else:
                ncol = tk
                s = s_ref[rows, :]
                s_max = x_ref[rows, :]
            vc = v_ref[pl.ds(c0, ncol), :]
            m_old = m_ref[rows, :]
            m_new = jnp.maximum(m_old, s_max)
            m_ref[rows, :] = m_new
            p = jnp.exp2(s - m_new).astype(BF16)
            acc_ref[rows, :] = (jnp.exp2(m_old - m_new) * acc_ref[rows, :]
                                + jnp.dot(p, vc, preferred_element_type=F32))

    def finalize():
        for r in range(nsub):
            rows = slice(r * tr, (r + 1) * tr)
            gate = jax.nn.sigmoid(og_ref[rows, :].astype(F32))
            inv_l = 1.0 / acc_ref[rows, FOX_HEAD:FOX_HEAD + 1]
            o_ref[rows, :] = (acc_ref[rows, :FOX_HEAD] * inv_l * gate).astype(o_ref.dtype)

    def pair(j, carry):
        c = 2 * j
        logits_to(c + 1, sb_ref, xb_ref)
        softmax_pv(c, sa_ref, xa_ref, False)
        logits_to(c + 2, sa_ref, xa_ref)
        softmax_pv(c + 1, sb_ref, xb_ref, False)
        return carry

    logits_to(0, sa_ref, xa_ref)
    lax.fori_loop(0, i // 2, pair, 0)

    @pl.when(i % 2 == 0)
    def _():
        softmax_pv(i, sa_ref, xa_ref, True)
        finalize()

    @pl.when(i % 2 == 1)
    def _():
        logits_to(i, sb_ref, xb_ref)
        softmax_pv(i - 1, sa_ref, xa_ref, False)
        softmax_pv(i, sb_ref, xb_ref, True)
        finalize()


def _fox_attention(qg, k_aug, v_aug, D):
    S = qg.shape[0]
    H = D // FOX_HEAD
    tq = tk = min(FOX_BLOCK, S)
    return pl.pallas_call(
        functools.partial(_fox_body, tq=tq, tk=tk, tr=min(FOX_SUB, tq)),
        grid=(H, S // tq),
        in_specs=[
            pl.BlockSpec((tq, FOX_HEAD), lambda h, i: (i, h)),
            pl.BlockSpec((S, 2 * LANES), lambda h, i: (0, h)),
            pl.BlockSpec((S, 2 * LANES), lambda h, i: (0, h)),
            pl.BlockSpec((tq, FOX_HEAD), lambda h, i: (i, H + h)),
        ],
        out_specs=pl.BlockSpec((tq, FOX_HEAD), lambda h, i: (i, h)),
        out_shape=jax.ShapeDtypeStruct((S, D), BF16),
        scratch_shapes=[pltpu.VMEM((tq, 2 * LANES), BF16), pltpu.VMEM((tq, tk), F32), pltpu.VMEM((tq, tk), F32),
                        pltpu.VMEM((tq, 1), F32), pltpu.VMEM((tq, 1), F32),
                        pltpu.VMEM((tq, 1), F32), pltpu.VMEM((tq, 2 * FOX_HEAD), F32)],
        compiler_params=_params("arbitrary", "arbitrary"),
        name="fox_attention",
    )(qg, k_aug, v_aug, qg)


def _pad_cols(w, n):
    return jnp.pad(w, ((0, 0), (0, n - w.shape[1])))


def _mlp(x, gain, shift, scale, gate, w_up, w_down, layer):
    h = _norm_mod(x, gain, shift, scale)
    u = _matmul(h, _Weight(w_up, layer), out_dtype=BF16, epilogue=_ep_sqrelu, name=f"mlp_up_l{layer}")
    return _matmul_resid(u, _Weight(w_down, layer), x, gate, name=f"mlp_down_l{layer}")


def kernel(x, c, ada_w, ada_b, norm_mix, norm_mlp, w_mlp_up, w_mlp_down, gla_w_in, gla_w_alpha_up,
           gla_b_alpha, gla_o_norm, gla_w_out, kv_ada_w, kv_ada_b, kv_norm, kv_w, fox_b_f, fox_k_norm,
           fox_w_in, fox_q_norm, fox_w_out):
    B, S, D = x.shape
    assert B == 1 and ada_w.shape[0] == 2
    xs = x[0]
    c_col = c.reshape(D, 1)
    w_down = _cast_bf16(w_mlp_down)
    KW = (D // GLA_DV) * GLA_DK

    sh1, sc1, g1, sh2, sc2, g2 = jnp.split(_ada(c_col, ada_w, ada_b[0][None], 0), 6, axis=1)
    h = _norm_mod(xs, norm_mix[0][None], sh1, sc1)
    n_main = 2 * KW + 2 * D
    proj = _matmul(h, _Weight(gla_w_in[0, :, :n_main].astype(BF16)), out_dtype=BF16, name="gla_in")
    a_lo = _matmul(h, _Weight(_pad_cols(gla_w_in[0, :, n_main:], LANES)), out_dtype=F32, name="gla_in_lowrank")
    w_up = jnp.pad(gla_w_alpha_up[0], ((0, LANES - gla_w_alpha_up.shape[1]), (0, 0))).astype(BF16)
    y = _gla(proj, a_lo, w_up, gla_b_alpha[0][None], gla_o_norm[0][None], D)
    xs = _matmul_resid(y, _Weight(gla_w_out, 0), xs, g1, name="gla_out")
    xs = _mlp(xs, norm_mlp[0][None], sh2, sc2, g2, w_mlp_up, w_down, 0)

    kshift, kscale = jnp.split(_ada(c_col, kv_ada_w, kv_ada_b[None], 0), 2, axis=1)
    hk = _norm_mod(xs, kv_norm[None], kshift, kscale)
    kv = _matmul(hk, _Weight(kv_w[:, :2 * D].astype(BF16)), out_dtype=BF16, name="kv_proj")
    f_logit = _matmul(hk, _Weight(_pad_cols(kv_w[:, 2 * D:], LANES)), out_dtype=F32, name="kv_forget")
    k_aug, v_aug = _fox_kv_augment(f_logit, _pad_cols(fox_b_f[None], LANES), fox_k_norm[None], kv, D)

    sh1, sc1, g1, sh2, sc2, g2 = jnp.split(_ada(c_col, ada_w, ada_b[1][None], 1), 6, axis=1)
    h = _norm_mod(xs, norm_mix[1][None], sh1, sc1)
    qg = _matmul_headnorm(h, _Weight(fox_w_in, 0), fox_q_norm[0][None], D, FOX_HEAD ** -0.5 * LOG2E,
                          name="fox_in")
    o = _fox_attention(qg, k_aug, v_aug, D)
    xs = _matmul_resid(o, _Weight(fox_w_out, 0), xs, g1, name="fox_out")
    xs = _mlp(xs, norm_mlp[1][None], sh2, sc2, g2, w_mlp_up, w_down, 1)
    return xs[None]
```

```python
import functools
from typing import NamedTuple

import numpy as np
import jax
import jax.numpy as jnp
from jax import lax
from jax.experimental import pallas as pl
from jax.experimental.pallas import tpu as pltpu

EPS = 1e-6
LANES = 128
BF16_SUBLANES = 16
GLA_DK = 256
GLA_DV = 512
GLA_TAU = 16.0
GLA_CHUNK = 128
GLA_HEADS_PER_STEP = 8
FOX_HEAD = 128
FOX_BLOCK = 1024
FOX_GROUPS = 2
FOX_SUB = 512
NEG_BIG = -1e30
VMEM_LIMIT_BYTES = 56 * 1024 * 1024
MM_TILE_K = 2048
MM_FULL_K = 4096
MM_VMEM_BUDGET = 44 * 1024 * 1024
MM_MIN_TILE_N = 256
LOG2E = 1.4426950408889634

F32 = jnp.float32
BF16 = jnp.bfloat16


def _params(*sem):
    return pltpu.CompilerParams(dimension_semantics=sem, vmem_limit_bytes=VMEM_LIMIT_BYTES)


def _log_sigmoid(x):
    return jnp.minimum(x, 0.0) - jnp.log1p(jnp.exp(-jnp.abs(x)))


def _silu(x):
    return x * jax.nn.sigmoid(x)


def _ada_body(c_ref, w_ref, b_ref, o_ref, cb_ref, *, tn):
    @pl.when(pl.program_id(0) == 0)
    def _():
        cc = c_ref[...]
        cb_ref[...] = jnp.broadcast_to(_silu(cc), cb_ref.shape)
    cb = cb_ref[...]
    parts = []
    for a in range(tn // LANES):
        parts.append(jnp.sum(w_ref[:, a * LANES:(a + 1) * LANES] * cb, axis=0, keepdims=True))
    o_ref[...] = jnp.concatenate(parts, axis=1) + b_ref[...]


def _ada(c_col, w, b_row, layer):
    D = c_col.shape[0]
    N = w.shape[-1]
    tn = min(512, N)
    if w.ndim == 3:
        w_spec = pl.BlockSpec((None, D, tn), lambda j: (layer, 0, j))
    else:
        w_spec = pl.BlockSpec((D, tn), lambda j: (0, j))
    return pl.pallas_call(
        functools.partial(_ada_body, tn=tn),
        grid=(N // tn,),
        in_specs=[pl.BlockSpec((D, 1), lambda j: (0, 0)), w_spec,
                  pl.BlockSpec((1, tn), lambda j: (0, j))],
        out_specs=pl.BlockSpec((1, tn), lambda j: (0, j)),
        out_shape=jax.ShapeDtypeStruct((1, N), F32),
        scratch_shapes=[pltpu.VMEM((D, LANES), F32)],
        compiler_params=_params("arbitrary"),
        name="ada_gemv",
    )(c_col, w, b_row)


def _normmod_body(x_ref, g_ref, sh_ref, sc_ref, o_ref):
    x = x_ref[...]
    ms = jnp.mean(x * x, axis=-1, keepdims=True)
    y = x * lax.rsqrt(ms + EPS) * g_ref[...]
    o_ref[...] = (y * (1.0 + sc_ref[...]) + sh_ref[...]).astype(o_ref.dtype)


def _norm_mod(x, gain, shift, scale):
    S, D = x.shape
    tr = min(512, S)
    row = pl.BlockSpec((1, D), lambda i: (0, 0))
    return pl.pallas_call(
        _normmod_body,
        grid=(S // tr,),
        in_specs=[pl.BlockSpec((tr, D), lambda i: (i, 0)), row, row, row],
        out_specs=pl.BlockSpec((tr, D), lambda i: (i, 0)),
        out_shape=jax.ShapeDtypeStruct((S, D), BF16),
        compiler_params=_params("parallel"),
        name="norm_mod",
    )(x, gain, shift, scale)


def _ep_cast(acc, extra, o_ref):
    o_ref[...] = acc.astype(o_ref.dtype)


def _ep_sqrelu(acc, extra, o_ref):
    r = jnp.maximum(acc, 0.0)
    o_ref[...] = (r * r).astype(o_ref.dtype)


def _ep_resid(acc, extra, o_ref):
    x_ref, g_ref = extra
    o_ref[...] = x_ref[...] + g_ref[...] * acc


def _ep_headnorm(acc, extra, o_ref, *, n_norm_blocks, scale):
    (gain_ref,) = extra
    j = pl.program_id(1)

    @pl.when(j < n_norm_blocks)
    def _():
        gain = gain_ref[...]
        for g in range(acc.shape[1] // FOX_HEAD):
            blk = acc[:, g * FOX_HEAD:(g + 1) * FOX_HEAD]
            ms = jnp.mean(blk * blk, axis=-1, keepdims=True)
            y = blk * lax.rsqrt(ms + EPS) * gain * scale
            o_ref[:, g * FOX_HEAD:(g + 1) * FOX_HEAD] = y.astype(o_ref.dtype)

    @pl.when(j >= n_norm_blocks)
    def _():
        o_ref[...] = acc.astype(o_ref.dtype)


def _mm_body(*refs, nk, n_extra, epilogue):
    a_ref, b_ref = refs[0], refs[1]
    extra = refs[2:2 + n_extra]
    o_ref = refs[2 + n_extra]
    if nk == 1:
        epilogue(jnp.dot(a_ref[...], b_ref[...].astype(BF16), preferred_element_type=F32), extra, o_ref)
        return
    acc_ref = refs[3 + n_extra]
    k = pl.program_id(2)

    @pl.when(k == 0)
    def _():
        acc_ref[...] = jnp.zeros_like(acc_ref)

    acc_ref[...] += jnp.dot(a_ref[...], b_ref[...].astype(BF16), preferred_element_type=F32)

    @pl.when(k == nk - 1)
    def _():
        epilogue(acc_ref[...], extra, o_ref)


class _Weight(NamedTuple):
    array: jax.Array
    layer: int | None = None

    @property
    def shape(self):
        return self.array.shape[-2:]


def _mm_tiles(M, w, out_dtype, resid=False):
    K, N = w.shape
    tk = K if K <= MM_FULL_K else MM_TILE_K
    nk = K // tk
    out_bytes = jnp.dtype(out_dtype).itemsize
    w_bytes = w.array.dtype.itemsize
    best = None
    for tm in (1024, 512, 256):
        for tn in (1024, 512, 256, LANES):
            if tm > M or M % tm or tn > N or N % tn or (tn < MM_MIN_TILE_N and tn < N):
                continue
            need = 2 * tm * tk * 2 + 2 * tk * tn * w_bytes + 2 * tm * tn * out_bytes
            need += (2 * tm * tn * 4 if resid else 0) + (tm * tn * 4 if nk > 1 else 0)
            if need > MM_VMEM_BUDGET:
                continue
            traffic = M * K * 2 * (N // tn if nk > 1 else 1) + K * N * w_bytes * (M // tm)
            if best is None or (traffic, -tn) < best[0]:
                best = ((traffic, -tn), (tm, tn, tk))
    return best[1]


def _matmul(a, w, *, out_dtype, epilogue=_ep_cast, extras=(), extra_specs=(), tiles=None, name):
    M, K = a.shape
    N = w.shape[1]
    tm, tn, tk = tiles or _mm_tiles(M, w, out_dtype)
    nk = K // tk
    scratch = [pltpu.VMEM((tm, tn), F32)] if nk > 1 else []
    if w.array.ndim == 3:
        w_spec = pl.BlockSpec((None, tk, tn), lambda i, j, k: (w.layer, k, j))
    else:
        w_spec = pl.BlockSpec((tk, tn), lambda i, j, k: (k, j))
    return pl.pallas_call(
        functools.partial(_mm_body, nk=nk, n_extra=len(extras), epilogue=epilogue),
        grid=(M // tm, N // tn, nk),
        in_specs=[pl.BlockSpec((tm, tk), lambda i, j, k: (i, k)), w_spec, *extra_specs],
        out_specs=pl.BlockSpec((tm, tn), lambda i, j, k: (i, j)),
        out_shape=jax.ShapeDtypeStruct((M, N), out_dtype),
        scratch_shapes=scratch,
        compiler_params=_params("parallel", "parallel", "arbitrary"),
        name=name,
    )(a, w.array, *extras)


def _matmul_resid(a, w, x, gate, name):
    M, N = x.shape
    tm, tn, _ = tiles = _mm_tiles(M, w, F32, resid=True)
    return _matmul(a, w, out_dtype=F32, epilogue=_ep_resid, extras=(x, gate),
                   extra_specs=(pl.BlockSpec((tm, tn), lambda i, j, k: (i, j)),
                                pl.BlockSpec((1, tn), lambda i, j, k: (0, j))), tiles=tiles, name=name)


def _matmul_headnorm(a, w, gain, n_norm_cols, scale, name):
    tiles = _mm_tiles(a.shape[0], w, BF16)
    ep = functools.partial(_ep_headnorm, n_norm_blocks=n_norm_cols // tiles[1], scale=scale)
    return _matmul(a, w, out_dtype=BF16, epilogue=ep, extras=(gain,),
                   extra_specs=(pl.BlockSpec((1, FOX_HEAD), lambda i, j, k: (0, 0)),), tiles=tiles, name=name)


def _gla_prefix_matrix(C):
    L = int(np.log2(C))
    t = np.arange(C)[:, None]
    j = np.arange(C)[None, :]
    mats = []
    for l in range(L):
        n = 1 << l
        mid = ((t >> (l + 1)) << (l + 1)) + n
        upper = ((t >> l) & 1) == 1
        mats.append(np.where(upper, (j >= mid) & (j <= t), (j > t) & (j < mid)))
    mats.append(j <= t)
    mats.append(j > t)
    return np.concatenate(mats, axis=0).astype(np.float32)


def _gla_body(*refs, C, L, HP, n_cast):
    q_ref, k_ref, v_ref, r_ref, alo_ref, wup_ref, balpha_ref, onorm_ref, f_ref = refs[:9]
    cast_in = refs[9:9 + n_cast]
    o_ref = refs[9 + n_cast]
    cast_out = refs[10 + n_cast:10 + 2 * n_cast]
    state_ref = refs[10 + 2 * n_cast]
    c = pl.program_id(0)
    hb = pl.program_id(1)

    for src, dst in zip(cast_in, cast_out):
        dst[...] = src[...].astype(BF16)

    row = lax.broadcasted_iota(jnp.int32, (C, C), 0)
    col = lax.broadcasted_iota(jnp.int32, (C, C), 1)
    a_lo = alo_ref[...].astype(BF16)
    fmat = f_ref[...]

    heads = range(HP)
    kcols = [slice(j * GLA_DK, (j + 1) * GLA_DK) for j in heads]
    vcols = [slice(j * GLA_DV, (j + 1) * GLA_DV) for j in heads]

    @pl.when(c == 0)
    def _():
        for j in heads:
            state_ref[hb * HP + j] = jnp.zeros(state_ref.shape[1:], F32)

    st = [state_ref[hb * HP + j] for j in heads]
    q = [q_ref[:, kcols[j]].astype(F32) * (GLA_DK ** -0.5) for j in heads]
    k = [k_ref[:, kcols[j]].astype(F32) for j in heads]
    v = [v_ref[:, vcols[j]] for j in heads]

    xs = []
    for j in heads:
        z = jnp.dot(a_lo, wup_ref[:, kcols[j]], preferred_element_type=F32) + balpha_ref[:, kcols[j]]
        la = _log_sigmoid(z) * (1.0 / GLA_TAU)
        hi = la.astype(BF16)
        mid = (la - hi.astype(F32)).astype(BF16)
        x = jnp.dot(fmat, jnp.concatenate([hi, mid], axis=1), preferred_element_type=F32)
        xs.append(x[:, :GLA_DK] + x[:, GLA_DK:])

    scores = [jnp.where(row == col, jnp.sum(q[j] * k[j], axis=-1, keepdims=True), 0.0) for j in heads]
    for l in range(L):
        rl = row >> l
        cl = col >> l
        pick = ((rl ^ cl) == 1) & (rl > cl)
        for j in heads:
            e = jnp.exp(xs[j][l * C:(l + 1) * C])
            s_l = lax.dot_general((q[j] * e).astype(BF16), (k[j] * e).astype(BF16),
                                  (((1,), (1,)), ((), ())), preferred_element_type=F32)
            scores[j] = jnp.where(pick, s_l, scores[j])

    outs = []
    for j in heads:
        b = xs[j][L * C:(L + 1) * C]
        rem = xs[j][(L + 1) * C:]
        o = jnp.dot(scores[j].astype(BF16), v[j], preferred_element_type=F32)
        o += jnp.dot((q[j] * jnp.exp(b)).astype(BF16), st[j].astype(BF16), preferred_element_type=F32)
        outs.append(o)
        kt = jnp.transpose(k[j] * jnp.exp(rem)).astype(BF16)
        d_row = jnp.exp(b[C - 1:C, :])
        d_col = jnp.transpose(jnp.broadcast_to(d_row, (LANES, GLA_DK)))[:, :1]
        st[j] = d_col * st[j] + jnp.dot(kt, v[j], preferred_element_type=F32)

    for j in heads:
        state_ref[hb * HP + j] = st[j]
        o = outs[j]
        ms = jnp.mean(o * o, axis=-1, keepdims=True)
        y = o * lax.rsqrt(ms + EPS) * onorm_ref[...]
        o_ref[:, vcols[j]] = (y * _silu(r_ref[:, vcols[j]].astype(F32))).astype(o_ref.dtype)


def _gla(proj, a_lo, w_up, b_alpha, o_norm, D, cast_srcs=()):
    S = proj.shape[0]
    H = D // GLA_DV
    KW = H * GLA_DK
    C = min(GLA_CHUNK, S)
    NC = S // C
    L = int(np.log2(C))
    HP = min(GLA_HEADS_PER_STEP, H)
    fmat = jnp.asarray(_gla_prefix_matrix(C), BF16)
    wk, wv = HP * GLA_DK, HP * GLA_DV
    kq, kv, kr = KW // wk, (2 * KW) // wv, (2 * KW + D) // wv
    cast_specs = []
    for w in cast_srcs:
        rows = w.shape[0] // NC
        assert rows * NC == w.shape[0] and rows % BF16_SUBLANES == 0, w.shape
        cast_specs.append(pl.BlockSpec((rows, w.shape[1]), lambda c, h: (c, 0)))
    outs = pl.pallas_call(
        functools.partial(_gla_body, C=C, L=L, HP=HP, n_cast=len(cast_srcs)),
        grid=(NC, H // HP),
        in_specs=[
            pl.BlockSpec((C, wk), lambda c, h: (c, h)),
            pl.BlockSpec((C, wk), lambda c, h: (c, kq + h)),
            pl.BlockSpec((C, wv), lambda c, h: (c, kv + h)),
            pl.BlockSpec((C, wv), lambda c, h: (c, kr + h)),
            pl.BlockSpec((C, LANES), lambda c, h: (c, 0)),
            pl.BlockSpec((LANES, wk), lambda c, h: (0, h)),
            pl.BlockSpec((1, wk), lambda c, h: (0, h)),
            pl.BlockSpec((1, GLA_DV), lambda c, h: (0, 0)),
            pl.BlockSpec(fmat.shape, lambda c, h: (0, 0)),
            *cast_specs,
        ],
        out_specs=[pl.BlockSpec((C, wv), lambda c, h: (c, h)), *cast_specs],
        out_shape=[jax.ShapeDtypeStruct((S, D), BF16), *(jax.ShapeDtypeStruct(w.shape, BF16) for w in cast_srcs)],
        scratch_shapes=[pltpu.VMEM((H, GLA_DK, GLA_DV), F32)],
        compiler_params=_params("arbitrary", "arbitrary"),
        name="gla_chunk",
    )(proj, proj, proj, proj, a_lo, w_up, b_alpha, o_norm, fmat, *cast_srcs)
    return outs[0], outs[1:]


def _split3(x):
    hi = x.astype(BF16)
    r1 = x - hi.astype(F32)
    mid = r1.astype(BF16)
    lo = (r1 - mid.astype(F32)).astype(BF16)
    return hi, mid, lo


def _kvaug_body(f_ref, bf_ref, tri_ref, kn_ref, k_ref, v_ref, ka_ref, va_ref, carry_ref, *, T, H):
    @pl.when(pl.program_id(0) == 0)
    def _():
        carry_ref[...] = jnp.zeros_like(carry_ref)

    lf = _log_sigmoid(f_ref[...] + bf_ref[...])
    cs = jnp.dot(tri_ref[...], jnp.concatenate(_split3(lf), axis=1), preferred_element_type=F32)
    cum = cs[:, :LANES] + cs[:, LANES:2 * LANES] + cs[:, 2 * LANES:] + carry_ref[...]
    carry_ref[...] = cum[T - 1:T, :]
    lane = lax.broadcasted_iota(jnp.int32, (T, LANES), 1)
    v_tail = jnp.where(lane == 0, 1.0, 0.0).astype(BF16)
    for h in range(H):
        val = cum[:, h:h + 1] * (-LOG2E)
        hi = val.astype(BF16).astype(F32)
        mid = (val - hi).astype(BF16).astype(F32)
        lo = val - hi - mid
        k_tail = jnp.where(lane == 0, hi, jnp.where(lane == 1, mid, jnp.where(lane == 2, lo, 0.0)))
        head = slice(h * FOX_HEAD, (h + 1) * FOX_HEAD)
        kh = k_ref[:, head].astype(F32)
        kh = kh * lax.rsqrt(jnp.mean(kh * kh, axis=-1, keepdims=True) + EPS) * kn_ref[...]
        ka_ref[:, 2 * h * LANES:(2 * h + 1) * LANES] = kh.astype(BF16)
        ka_ref[:, (2 * h + 1) * LANES:(2 * h + 2) * LANES] = k_tail.astype(BF16)
        va_ref[:, 2 * h * LANES:(2 * h + 1) * LANES] = v_ref[:, head]
        va_ref[:, (2 * h + 1) * LANES:(2 * h + 2) * LANES] = v_tail


def _fox_kv_augment(f_logit, b_f, k_gain, kv, D):
    S = f_logit.shape[0]
    H = D // FOX_HEAD
    T = min(256, S)
    tri = jnp.asarray(np.tril(np.ones((T, T), np.float32)), BF16)
    aug = jax.ShapeDtypeStruct((S, 2 * D), BF16)
    return pl.pallas_call(
        functools.partial(_kvaug_body, T=T, H=H),
        grid=(S // T,),
        in_specs=[pl.BlockSpec((T, LANES), lambda i: (i, 0)),
                  pl.BlockSpec((1, LANES), lambda i: (0, 0)),
                  pl.BlockSpec((T, T), lambda i: (0, 0)),
                  pl.BlockSpec((1, FOX_HEAD), lambda i: (0, 0)),
                  pl.BlockSpec((T, D), lambda i: (i, 0)),
                  pl.BlockSpec((T, D), lambda i: (i, 1))],
        out_specs=[pl.BlockSpec((T, 2 * D), lambda i: (i, 0)), pl.BlockSpec((T, 2 * D), lambda i: (i, 0))],
        out_shape=[aug, aug],
        scratch_shapes=[pltpu.VMEM((1, LANES), F32)],
        compiler_params=_params("arbitrary"),
        name="fox_kv_augment",
    )(f_logit, b_f, tri, k_gain, kv, kv)


def _fox_body(q_ref, k_ref, v_ref, og_ref, o_ref, qa_ref, sa_ref, sb_ref, xa_ref, xb_ref, m_ref, acc_ref,
              *, tk, tr, G):
    i = pl.program_id(1)
    nsub = tk // tr
    bufs = ((sa_ref, xa_ref), (sb_ref, xb_ref))
    lane_q = lax.broadcasted_iota(jnp.int32, (G * tk, LANES), 1)
    qa_ref[:, :FOX_HEAD] = q_ref[...]
    qa_ref[:, FOX_HEAD:] = jnp.where(lane_q < 3, 1.0, 0.0).astype(BF16)
    m_ref[...] = jnp.full(m_ref.shape, NEG_BIG, F32)
    acc_ref[...] = jnp.zeros(acc_ref.shape, F32)

    def logits(g, cidx, buf):
        s_ref, x_ref = bufs[buf]
        kc = k_ref[pl.ds(pl.multiple_of(cidx * tk, tk), tk), :]
        for r in range(nsub):
            rows = slice(r * tr, (r + 1) * tr)
            s = lax.dot_general(qa_ref[g * tk + r * tr:g * tk + (r + 1) * tr, :], kc, (((1,), (1,)), ((), ())),
                                preferred_element_type=F32)
            s_ref[rows, :] = s
            x_ref[rows, :] = jnp.max(s, axis=-1, keepdims=True)

    def softmax_pv(g, cidx, buf, masked):
        s_ref, x_ref = bufs[buf]
        c0 = pl.multiple_of(cidx * tk, tk)
        for r in range(nsub):
            rows = slice(r * tr, (r + 1) * tr)
            grows = slice(g * tk + r * tr, g * tk + (r + 1) * tr)
            if masked:
                ncol = (r + 1) * tr
                row = lax.broadcasted_iota(jnp.int32, (tr, ncol), 0) + r * tr
                col = lax.broadcasted_iota(jnp.int32, (tr, ncol), 1)
                s = jnp.where(row >= col, s_ref[rows, :ncol], NEG_BIG)
                s_max = jnp.max(s, axis=-1, keepdims=True)
            else:
                ncol = tk
                s = s_ref[rows, :]
                s_max = x_ref[rows, :]
            vc = v_ref[pl.ds(c0, ncol), :]
            m_old = m_ref[grows, :]
            m_new = jnp.maximum(m_old, s_max)
            m_ref[grows, :] = m_new
            p = jnp.exp2(s - m_new).astype(BF16)
            acc_ref[grows, :] = (jnp.exp2(m_old - m_new) * acc_ref[grows, :]
                                 + jnp.dot(p, vc, preferred_element_type=F32))

    def full_blocks(cidx, carry):
        for g in range(G):
            if g + 1 < G:
                logits(g + 1, cidx, (g + 1) % 2)
            else:
                logits(0, cidx + 1, 0)
            softmax_pv(g, cidx, g % 2, False)
        return carry

    logits(0, 0, 0)
    lax.fori_loop(0, G * i, full_blocks, 0)
    tail = [(g, G * i + d, g == d) for d in range(G) for g in range(d, G)]
    for t, (g, cidx, masked) in enumerate(tail):
        if t + 1 < len(tail):
            logits(tail[t + 1][0], tail[t + 1][1], (t + 1) % 2)
        softmax_pv(g, cidx, t % 2, masked)

    for g in range(G):
        for r in range(nsub):
            rows = slice(g * tk + r * tr, g * tk + (r + 1) * tr)
            gate = jax.nn.sigmoid(og_ref[rows, :].astype(F32))
            inv_l = 1.0 / acc_ref[rows, FOX_HEAD:FOX_HEAD + 1]
            o_ref[rows, :] = (acc_ref[rows, :FOX_HEAD] * inv_l * gate).astype(o_ref.dtype)


def _fox_attention(qg, k_aug, v_aug, D):
    S = qg.shape[0]
    H = D // FOX_HEAD
    G = FOX_GROUPS
    tk = min(FOX_BLOCK, S // G)
    tq = G * tk
    return pl.pallas_call(
        functools.partial(_fox_body, tk=tk, tr=min(FOX_SUB, tk), G=G),
        grid=(H, S // tq),
        in_specs=[
            pl.BlockSpec((tq, FOX_HEAD), lambda h, i: (i, h)),
            pl.BlockSpec((S, 2 * LANES), lambda h, i: (0, h)),
            pl.BlockSpec((S, 2 * LANES), lambda h, i: (0, h)),
            pl.BlockSpec((tq, FOX_HEAD), lambda h, i: (i, H + h)),
        ],
        out_specs=pl.BlockSpec((tq, FOX_HEAD), lambda h, i: (i, h)),
        out_shape=jax.ShapeDtypeStruct((S, D), BF16),
        scratch_shapes=[pltpu.VMEM((tq, 2 * LANES), BF16), pltpu.VMEM((tk, tk), F32), pltpu.VMEM((tk, tk), F32),
                        pltpu.VMEM((tk, 1), F32), pltpu.VMEM((tk, 1), F32),
                        pltpu.VMEM((tq, 1), F32), pltpu.VMEM((tq, 2 * FOX_HEAD), F32)],
        compiler_params=_params("arbitrary", "arbitrary"),
        name="fox_attention",
    )(qg, k_aug, v_aug, qg)


def _pad_cols(w, n):
    return jnp.pad(w, ((0, 0), (0, n - w.shape[1])))


def _mlp(x, gain, shift, scale, gate, w_up, w_down, layer):
    h = _norm_mod(x, gain, shift, scale)
    u = _matmul(h, _Weight(w_up, layer), out_dtype=BF16, epilogue=_ep_sqrelu, name=f"mlp_up_l{layer}")
    return _matmul_resid(u, _Weight(w_down, layer), x, gate, name=f"mlp_down_l{layer}")


def kernel(x, c, ada_w, ada_b, norm_mix, norm_mlp, w_mlp_up, w_mlp_down, gla_w_in, gla_w_alpha_up,
           gla_b_alpha, gla_o_norm, gla_w_out, kv_ada_w, kv_ada_b, kv_norm, kv_w, fox_b_f, fox_k_norm,
           fox_w_in, fox_q_norm, fox_w_out):
    B, S, D = x.shape
    assert B == 1 and ada_w.shape[0] == 2
    xs = x[0]
    c_col = c.reshape(D, 1)
    KW = (D // GLA_DV) * GLA_DK

    sh1, sc1, g1, sh2, sc2, g2 = jnp.split(_ada(c_col, ada_w, ada_b[0][None], 0), 6, axis=1)
    h = _norm_mod(xs, norm_mix[0][None], sh1, sc1)
    n_main = 2 * KW + 2 * D
    proj = _matmul(h, _Weight(gla_w_in[0, :, :n_main].astype(BF16)), out_dtype=BF16, name="gla_in")
    a_lo = _matmul(h, _Weight(_pad_cols(gla_w_in[0, :, n_main:], LANES)), out_dtype=F32, name="gla_in_lowrank")
    w_up = jnp.pad(gla_w_alpha_up[0], ((0, LANES - gla_w_alpha_up.shape[1]), (0, 0))).astype(BF16)
    y, (w_up_b, w_down_b) = _gla(proj, a_lo, w_up, gla_b_alpha[0][None], gla_o_norm[0][None], D,
                                 cast_srcs=(w_mlp_up.reshape(-1, w_mlp_up.shape[-1]),
                                            w_mlp_down.reshape(-1, w_mlp_down.shape[-1])))
    w_up_b = w_up_b.reshape(w_mlp_up.shape)
    w_down_b = w_down_b.reshape(w_mlp_down.shape)
    xs = _matmul_resid(y, _Weight(gla_w_out, 0), xs, g1, name="gla_out")
    xs = _mlp(xs, norm_mlp[0][None], sh2, sc2, g2, w_up_b, w_down_b, 0)

    kshift, kscale = jnp.split(_ada(c_col, kv_ada_w, kv_ada_b[None], 0), 2, axis=1)
    hk = _norm_mod(xs, kv_norm[None], kshift, kscale)
    kv = _matmul(hk, _Weight(kv_w[:, :2 * D].astype(BF16)), out_dtype=BF16, name="kv_proj")
    f_logit = _matmul(hk, _Weight(_pad_cols(kv_w[:, 2 * D:], LANES)), out_dtype=F32, name="kv_forget")
    k_aug, v_aug = _fox_kv_augment(f_logit, _pad_cols(fox_b_f[None], LANES), fox_k_norm[None], kv, D)

    sh1, sc1, g1, sh2, sc2, g2 = jnp.split(_ada(c_col, ada_w, ada_b[1][None], 1), 6, axis=1)
    h = _norm_mod(xs, norm_mix[1][None], sh1, sc1)
    qg = _matmul_headnorm(h, _Weight(fox_w_in, 0), fox_q_norm[0][None], D, FOX_HEAD ** -0.5 * LOG2E,
                          name="fox_in")
    o = _fox_attention(qg, k_aug, v_aug, D)
    xs = _matmul_resid(o, _Weight(fox_w_out, 0), xs, g1, name="fox_out")
    xs = _mlp(xs, norm_mlp[1][None], sh2, sc2, g2, w_up_b, w_down_b, 1)
    return xs[None]
```

```python
import functools
from typing import NamedTuple

import numpy as np
import jax
import jax.numpy as jnp
from jax import lax
from jax.experimental import pallas as pl
from jax.experimental.pallas import tpu as pltpu

EPS = 1e-6
LANES = 128
BF16_SUBLANES = 16
GLA_DK = 256
GLA_DV = 512
GLA_TAU = 16.0
GLA_CHUNK = 128
GLA_HEADS_PER_STEP = 8
FOX_HEAD = 128
FOX_BLOCK = 1024
FOX_GROUPS = 2
FOX_SUB = 512
NEG_BIG = -1e30
VMEM_LIMIT_BYTES = 56 * 1024 * 1024
MM_TILE_K = 2048
MM_FULL_K = 4096
MM_VMEM_BUDGET = 44 * 1024 * 1024
MM_MIN_TILE_N = 256
LOG2E = 1.4426950408889634

F32 = jnp.float32
BF16 = jnp.bfloat16


def _params(*sem):
    return pltpu.CompilerParams(dimension_semantics=sem, vmem_limit_bytes=VMEM_LIMIT_BYTES)


def _log_sigmoid(x):
    return jnp.minimum(x, 0.0) - jnp.log1p(jnp.exp(-jnp.abs(x)))


def _silu(x):
    return x * jax.nn.sigmoid(x)


def _ada_body(c_ref, w_ref, b_ref, o_ref, cb_ref, *, tn):
    @pl.when(pl.program_id(0) == 0)
    def _():
        cc = c_ref[...]
        cb_ref[...] = jnp.broadcast_to(_silu(cc), cb_ref.shape)
    cb = cb_ref[...]
    parts = []
    for a in range(tn // LANES):
        parts.append(jnp.sum(w_ref[:, a * LANES:(a + 1) * LANES] * cb, axis=0, keepdims=True))
    o_ref[...] = jnp.concatenate(parts, axis=1) + b_ref[...]


def _ada(c_col, w, b_row, layer):
    D = c_col.shape[0]
    N = w.shape[-1]
    tn = min(512, N)
    if w.ndim == 3:
        w_spec = pl.BlockSpec((None, D, tn), lambda j: (layer, 0, j))
    else:
        w_spec = pl.BlockSpec((D, tn), lambda j: (0, j))
    return pl.pallas_call(
        functools.partial(_ada_body, tn=tn),
        grid=(N // tn,),
        in_specs=[pl.BlockSpec((D, 1), lambda j: (0, 0)), w_spec,
                  pl.BlockSpec((1, tn), lambda j: (0, j))],
        out_specs=pl.BlockSpec((1, tn), lambda j: (0, j)),
        out_shape=jax.ShapeDtypeStruct((1, N), F32),
        scratch_shapes=[pltpu.VMEM((D, LANES), F32)],
        compiler_params=_params("arbitrary"),
        name="ada_gemv",
    )(c_col, w, b_row)


def _normmod_body(x_ref, *refs):
    n = len(refs) // 4
    x = x_ref[...]
    y = x * lax.rsqrt(jnp.mean(x * x, axis=-1, keepdims=True) + EPS)
    for t in range(n):
        g_ref, sh_ref, sc_ref = refs[3 * t:3 * t + 3]
        o_ref = refs[3 * n + t]
        o_ref[...] = (y * g_ref[...] * (1.0 + sc_ref[...]) + sh_ref[...]).astype(o_ref.dtype)


def _norm_mod(x, *param_sets):
    S, D = x.shape
    tr = min(512, S)
    row = pl.BlockSpec((1, D), lambda i: (0, 0))
    tile = pl.BlockSpec((tr, D), lambda i: (i, 0))
    outs = pl.pallas_call(
        _normmod_body,
        grid=(S // tr,),
        in_specs=[tile] + [row] * (3 * len(param_sets)),
        out_specs=[tile] * len(param_sets),
        out_shape=[jax.ShapeDtypeStruct((S, D), BF16)] * len(param_sets),
        compiler_params=_params("parallel"),
        name="norm_mod",
    )(x, *(p for ps in param_sets for p in ps))
    return outs[0] if len(param_sets) == 1 else outs


def _ep_cast(acc, extra, o_ref):
    o_ref[...] = acc.astype(o_ref.dtype)


def _ep_sqrelu(acc, extra, o_ref):
    r = jnp.maximum(acc, 0.0)
    o_ref[...] = (r * r).astype(o_ref.dtype)


def _ep_resid(acc, extra, o_ref):
    x_ref, g_ref = extra
    o_ref[...] = x_ref[...] + g_ref[...] * acc


def _ep_headnorm(acc, extra, o_ref, *, n_norm_blocks, scale):
    (gain_ref,) = extra
    j = pl.program_id(1)

    @pl.when(j < n_norm_blocks)
    def _():
        gain = gain_ref[...]
        for g in range(acc.shape[1] // FOX_HEAD):
            blk = acc[:, g * FOX_HEAD:(g + 1) * FOX_HEAD]
            ms = jnp.mean(blk * blk, axis=-1, keepdims=True)
            y = blk * lax.rsqrt(ms + EPS) * gain * scale
            o_ref[:, g * FOX_HEAD:(g + 1) * FOX_HEAD] = y.astype(o_ref.dtype)

    @pl.when(j >= n_norm_blocks)
    def _():
        o_ref[...] = acc.astype(o_ref.dtype)


def _mm_body(*refs, nk, n_extra, epilogue):
    a_ref, b_ref = refs[0], refs[1]
    extra = refs[2:2 + n_extra]
    o_ref = refs[2 + n_extra]
    if nk == 1:
        epilogue(jnp.dot(a_ref[...], b_ref[...].astype(BF16), preferred_element_type=F32), extra, o_ref)
        return
    acc_ref = refs[3 + n_extra]
    k = pl.program_id(2)

    @pl.when(k == 0)
    def _():
        acc_ref[...] = jnp.zeros_like(acc_ref)

    acc_ref[...] += jnp.dot(a_ref[...], b_ref[...].astype(BF16), preferred_element_type=F32)

    @pl.when(k == nk - 1)
    def _():
        epilogue(acc_ref[...], extra, o_ref)


class _Weight(NamedTuple):
    array: jax.Array
    layer: int | None = None

    @property
    def shape(self):
        return self.array.shape[-2:]


def _mm_tiles(M, w, out_dtype, resid=False):
    K, N = w.shape
    tk = K if K <= MM_FULL_K else MM_TILE_K
    nk = K // tk
    out_bytes = jnp.dtype(out_dtype).itemsize
    w_bytes = w.array.dtype.itemsize
    best = None
    for tm in (1024, 512, 256):
        for tn in (1024, 512, 256, LANES):
            if tm > M or M % tm or tn > N or N % tn or (tn < MM_MIN_TILE_N and tn < N):
                continue
            need = 2 * tm * tk * 2 + 2 * tk * tn * w_bytes + 2 * tm * tn * out_bytes
            need += (2 * tm * tn * 4 if resid else 0) + (tm * tn * 4 if nk > 1 else 0)
            if need > MM_VMEM_BUDGET:
                continue
            traffic = M * K * 2 * (N // tn if nk > 1 else 1) + K * N * w_bytes * (M // tm)
            if best is None or (traffic, -tn) < best[0]:
                best = ((traffic, -tn), (tm, tn, tk))
    return best[1]


def _matmul(a, w, *, out_dtype, epilogue=_ep_cast, extras=(), extra_specs=(), tiles=None, name):
    M, K = a.shape
    N = w.shape[1]
    tm, tn, tk = tiles or _mm_tiles(M, w, out_dtype)
    nk = K // tk
    scratch = [pltpu.VMEM((tm, tn), F32)] if nk > 1 else []
    if w.array.ndim == 3:
        w_spec = pl.BlockSpec((None, tk, tn), lambda i, j, k: (w.layer, k, j))
    else:
        w_spec = pl.BlockSpec((tk, tn), lambda i, j, k: (k, j))
    return pl.pallas_call(
        functools.partial(_mm_body, nk=nk, n_extra=len(extras), epilogue=epilogue),
        grid=(M // tm, N // tn, nk),
        in_specs=[pl.BlockSpec((tm, tk), lambda i, j, k: (i, k)), w_spec, *extra_specs],
        out_specs=pl.BlockSpec((tm, tn), lambda i, j, k: (i, j)),
        out_shape=jax.ShapeDtypeStruct((M, N), out_dtype),
        scratch_shapes=scratch,
        compiler_params=_params("parallel", "parallel", "arbitrary"),
        name=name,
    )(a, w.array, *extras)


def _matmul_resid(a, w, x, gate, name):
    M, N = x.shape
    tm, tn, _ = tiles = _mm_tiles(M, w, F32, resid=True)
    return _matmul(a, w, out_dtype=F32, epilogue=_ep_resid, extras=(x, gate),
                   extra_specs=(pl.BlockSpec((tm, tn), lambda i, j, k: (i, j)),
                                pl.BlockSpec((1, tn), lambda i, j, k: (0, j))), tiles=tiles, name=name)


def _matmul_headnorm(a, w, gain, n_norm_cols, scale, name):
    tiles = _mm_tiles(a.shape[0], w, BF16)
    ep = functools.partial(_ep_headnorm, n_norm_blocks=n_norm_cols // tiles[1], scale=scale)
    return _matmul(a, w, out_dtype=BF16, epilogue=ep, extras=(gain,),
                   extra_specs=(pl.BlockSpec((1, FOX_HEAD), lambda i, j, k: (0, 0)),), tiles=tiles, name=name)


def _gla_prefix_matrix(C):
    L = int(np.log2(C))
    t = np.arange(C)[:, None]
    j = np.arange(C)[None, :]
    mats = []
    for l in range(L):
        n = 1 << l
        mid = ((t >> (l + 1)) << (l + 1)) + n
        upper = ((t >> l) & 1) == 1
        mats.append(np.where(upper, (j >= mid) & (j <= t), (j > t) & (j < mid)))
    mats.append(j <= t)
    mats.append(j > t)
    return np.concatenate(mats, axis=0).astype(np.float32)


def _gla_body(*refs, C, L, HP, n_cast):
    q_ref, k_ref, v_ref, r_ref, alo_ref, wup_ref, balpha_ref, onorm_ref, f_ref = refs[:9]
    cast_in = refs[9:9 + n_cast]
    o_ref = refs[9 + n_cast]
    cast_out = refs[10 + n_cast:10 + 2 * n_cast]
    state_ref = refs[10 + 2 * n_cast]
    c = pl.program_id(0)
    hb = pl.program_id(1)

    for src, dst in zip(cast_in, cast_out):
        dst[...] = src[...].astype(BF16)

    row = lax.broadcasted_iota(jnp.int32, (C, C), 0)
    col = lax.broadcasted_iota(jnp.int32, (C, C), 1)
    a_lo = alo_ref[...].astype(BF16)
    fmat = f_ref[...]

    heads = range(HP)
    kcols = [slice(j * GLA_DK, (j + 1) * GLA_DK) for j in heads]
    vcols = [slice(j * GLA_DV, (j + 1) * GLA_DV) for j in heads]

    @pl.when(c == 0)
    def _():
        for j in heads:
            state_ref[hb * HP + j] = jnp.zeros(state_ref.shape[1:], F32)

    st = [state_ref[hb * HP + j] for j in heads]
    q = [q_ref[:, kcols[j]].astype(F32) * (GLA_DK ** -0.5) for j in heads]
    k = [k_ref[:, kcols[j]].astype(F32) for j in heads]
    v = [v_ref[:, vcols[j]] for j in heads]

    xs = []
    for j in heads:
        z = jnp.dot(a_lo, wup_ref[:, kcols[j]], preferred_element_type=F32) + balpha_ref[:, kcols[j]]
        la = _log_sigmoid(z) * (1.0 / GLA_TAU)
        hi = la.astype(BF16)
        mid = (la - hi.astype(F32)).astype(BF16)
        x = jnp.dot(fmat, jnp.concatenate([hi, mid], axis=1), preferred_element_type=F32)
        xs.append(x[:, :GLA_DK] + x[:, GLA_DK:])

    scores = [jnp.where(row == col, jnp.sum(q[j] * k[j], axis=-1, keepdims=True), 0.0) for j in heads]
    for l in range(L):
        rl = row >> l
        cl = col >> l
        pick = ((rl ^ cl) == 1) & (rl > cl)
        for j in heads:
            e = jnp.exp(xs[j][l * C:(l + 1) * C])
            s_l = lax.dot_general((q[j] * e).astype(BF16), (k[j] * e).astype(BF16),
                                  (((1,), (1,)), ((), ())), preferred_element_type=F32)
            scores[j] = jnp.where(pick, s_l, scores[j])

    outs = []
    for j in heads:
        b = xs[j][L * C:(L + 1) * C]
        rem = xs[j][(L + 1) * C:]
        o = jnp.dot(scores[j].astype(BF16), v[j], preferred_element_type=F32)
        o += jnp.dot((q[j] * jnp.exp(b)).astype(BF16), st[j].astype(BF16), preferred_element_type=F32)
        outs.append(o)
        kt = jnp.transpose(k[j] * jnp.exp(rem)).astype(BF16)
        d_row = jnp.exp(b[C - 1:C, :])
        d_col = jnp.transpose(jnp.broadcast_to(d_row, (LANES, GLA_DK)))[:, :1]
        st[j] = d_col * st[j] + jnp.dot(kt, v[j], preferred_element_type=F32)

    for j in heads:
        state_ref[hb * HP + j] = st[j]
        o = outs[j]
        ms = jnp.mean(o * o, axis=-1, keepdims=True)
        y = o * lax.rsqrt(ms + EPS) * onorm_ref[...]
        o_ref[:, vcols[j]] = (y * _silu(r_ref[:, vcols[j]].astype(F32))).astype(o_ref.dtype)


def _gla(proj, a_lo, w_up, b_alpha, o_norm, D, cast_srcs=()):
    S = proj.shape[0]
    H = D // GLA_DV
    KW = H * GLA_DK
    C = min(GLA_CHUNK, S)
    NC = S // C
    L = int(np.log2(C))
    HP = min(GLA_HEADS_PER_STEP, H)
    fmat = jnp.asarray(_gla_prefix_matrix(C), BF16)
    wk, wv = HP * GLA_DK, HP * GLA_DV
    kq, kv, kr = KW // wk, (2 * KW) // wv, (2 * KW + D) // wv
    cast_specs = []
    for w in cast_srcs:
        rows = w.shape[0] // NC
        assert rows * NC == w.shape[0] and rows % BF16_SUBLANES == 0, w.shape
        cast_specs.append(pl.BlockSpec((rows, w.shape[1]), lambda c, h: (c, 0)))
    outs = pl.pallas_call(
        functools.partial(_gla_body, C=C, L=L, HP=HP, n_cast=len(cast_srcs)),
        grid=(NC, H // HP),
        in_specs=[
            pl.BlockSpec((C, wk), lambda c, h: (c, h)),
            pl.BlockSpec((C, wk), lambda c, h: (c, kq + h)),
            pl.BlockSpec((C, wv), lambda c, h: (c, kv + h)),
            pl.BlockSpec((C, wv), lambda c, h: (c, kr + h)),
            pl.BlockSpec((C, LANES), lambda c, h: (c, 0)),
            pl.BlockSpec((LANES, wk), lambda c, h: (0, h)),
            pl.BlockSpec((1, wk), lambda c, h: (0, h)),
            pl.BlockSpec((1, GLA_DV), lambda c, h: (0, 0)),
            pl.BlockSpec(fmat.shape, lambda c, h: (0, 0)),
            *cast_specs,
        ],
        out_specs=[pl.BlockSpec((C, wv), lambda c, h: (c, h)), *cast_specs],
        out_shape=[jax.ShapeDtypeStruct((S, D), BF16), *(jax.ShapeDtypeStruct(w.shape, BF16) for w in cast_srcs)],
        scratch_shapes=[pltpu.VMEM((H, GLA_DK, GLA_DV), F32)],
        compiler_params=_params("arbitrary", "arbitrary"),
        name="gla_chunk",
    )(proj, proj, proj, proj, a_lo, w_up, b_alpha, o_norm, fmat, *cast_srcs)
    return outs[0], outs[1:]


def _split3(x):
    hi = x.astype(BF16)
    r1 = x - hi.astype(F32)
    mid = r1.astype(BF16)
    lo = (r1 - mid.astype(F32)).astype(BF16)
    return hi, mid, lo


def _kvaug_body(f_ref, bf_ref, tri_ref, kn_ref, k_ref, v_ref, ka_ref, va_ref, carry_ref, *, T, H):
    @pl.when(pl.program_id(0) == 0)
    def _():
        carry_ref[...] = jnp.zeros_like(carry_ref)

    lf = _log_sigmoid(f_ref[...] + bf_ref[...])
    cs = jnp.dot(tri_ref[...], jnp.concatenate(_split3(lf), axis=1), preferred_element_type=F32)
    cum = cs[:, :LANES] + cs[:, LANES:2 * LANES] + cs[:, 2 * LANES:] + carry_ref[...]
    carry_ref[...] = cum[T - 1:T, :]
    lane = lax.broadcasted_iota(jnp.int32, (T, LANES), 1)
    v_tail = jnp.where(lane == 0, 1.0, 0.0).astype(BF16)
    for h in range(H):
        val = cum[:, h:h + 1] * (-LOG2E)
        hi = val.astype(BF16).astype(F32)
        mid = (val - hi).astype(BF16).astype(F32)
        lo = val - hi - mid
        k_tail = jnp.where(lane == 0, hi, jnp.where(lane == 1, mid, jnp.where(lane == 2, lo, 0.0)))
        head = slice(h * FOX_HEAD, (h + 1) * FOX_HEAD)
        kh = k_ref[:, head].astype(F32)
        kh = kh * lax.rsqrt(jnp.mean(kh * kh, axis=-1, keepdims=True) + EPS) * kn_ref[...]
        ka_ref[:, 2 * h * LANES:(2 * h + 1) * LANES] = kh.astype(BF16)
        ka_ref[:, (2 * h + 1) * LANES:(2 * h + 2) * LANES] = k_tail.astype(BF16)
        va_ref[:, 2 * h * LANES:(2 * h + 1) * LANES] = v_ref[:, head]
        va_ref[:, (2 * h + 1) * LANES:(2 * h + 2) * LANES] = v_tail


def _fox_kv_augment(f_logit, b_f, k_gain, kv, D):
    S = f_logit.shape[0]
    H = D // FOX_HEAD
    T = min(256, S)
    tri = jnp.asarray(np.tril(np.ones((T, T), np.float32)), BF16)
    aug = jax.ShapeDtypeStruct((S, 2 * D), BF16)
    return pl.pallas_call(
        functools.partial(_kvaug_body, T=T, H=H),
        grid=(S // T,),
        in_specs=[pl.BlockSpec((T, LANES), lambda i: (i, 0)),
                  pl.BlockSpec((1, LANES), lambda i: (0, 0)),
                  pl.BlockSpec((T, T), lambda i: (0, 0)),
                  pl.BlockSpec((1, FOX_HEAD), lambda i: (0, 0)),
                  pl.BlockSpec((T, D), lambda i: (i, 0)),
                  pl.BlockSpec((T, D), lambda i: (i, 1))],
        out_specs=[pl.BlockSpec((T, 2 * D), lambda i: (i, 0)), pl.BlockSpec((T, 2 * D), lambda i: (i, 0))],
        out_shape=[aug, aug],
        scratch_shapes=[pltpu.VMEM((1, LANES), F32)],
        compiler_params=_params("arbitrary"),
        name="fox_kv_augment",
    )(f_logit, b_f, tri, k_gain, kv, kv)


def _fox_body(q_ref, k_ref, v_ref, og_ref, o_ref, qa_ref, sa_ref, sb_ref, xa_ref, xb_ref, m_ref, acc_ref,
              *, tk, tr, G):
    i = pl.program_id(1)
    nsub = tk // tr
    bufs = ((sa_ref, xa_ref), (sb_ref, xb_ref))
    lane_q = lax.broadcasted_iota(jnp.int32, (G * tk, LANES), 1)
    qa_ref[:, :FOX_HEAD] = q_ref[...]
    qa_ref[:, FOX_HEAD:] = jnp.where(lane_q < 3, 1.0, 0.0).astype(BF16)
    m_ref[...] = jnp.full(m_ref.shape, NEG_BIG, F32)
    acc_ref[...] = jnp.zeros(acc_ref.shape, F32)

    def logits(g, cidx, buf):
        s_ref, x_ref = bufs[buf]
        kc = k_ref[pl.ds(pl.multiple_of(cidx * tk, tk), tk), :]
        for r in range(nsub):
            rows = slice(r * tr, (r + 1) * tr)
            s = lax.dot_general(qa_ref[g * tk + r * tr:g * tk + (r + 1) * tr, :], kc, (((1,), (1,)), ((), ())),
                                preferred_element_type=F32)
            s_ref[rows, :] = s
            x_ref[rows, :] = jnp.max(s, axis=-1, keepdims=True)

    def softmax_pv(g, cidx, buf, masked):
        s_ref, x_ref = bufs[buf]
        c0 = pl.multiple_of(cidx * tk, tk)
        for r in range(nsub):
            rows = slice(r * tr, (r + 1) * tr)
            grows = slice(g * tk + r * tr, g * tk + (r + 1) * tr)
            if masked:
                ncol = (r + 1) * tr
                row = lax.broadcasted_iota(jnp.int32, (tr, ncol), 0) + r * tr
                col = lax.broadcasted_iota(jnp.int32, (tr, ncol), 1)
                s = jnp.where(row >= col, s_ref[rows, :ncol], NEG_BIG)
                s_max = jnp.max(s, axis=-1, keepdims=True)
            else:
                ncol = tk
                s = s_ref[rows, :]
                s_max = x_ref[rows, :]
            vc = v_ref[pl.ds(c0, ncol), :]
            m_old = m_ref[grows, :]
            m_new = jnp.maximum(m_old, s_max)
            m_ref[grows, :] = m_new
            p = jnp.exp2(s - m_new).astype(BF16)
            acc_ref[grows, :] = (jnp.exp2(m_old - m_new) * acc_ref[grows, :]
                                 + jnp.dot(p, vc, preferred_element_type=F32))

    def full_blocks(cidx, carry):
        for g in range(G):
            if g + 1 < G:
                logits(g + 1, cidx, (g + 1) % 2)
            else:
                logits(0, cidx + 1, 0)
            softmax_pv(g, cidx, g % 2, False)
        return carry

    logits(0, 0, 0)
    lax.fori_loop(0, G * i, full_blocks, 0)
    tail = [(g, G * i + d, g == d) for d in range(G) for g in range(d, G)]
    for t, (g, cidx, masked) in enumerate(tail):
        if t + 1 < len(tail):
            logits(tail[t + 1][0], tail[t + 1][1], (t + 1) % 2)
        softmax_pv(g, cidx, t % 2, masked)

    for g in range(G):
        for r in range(nsub):
            rows = slice(g * tk + r * tr, g * tk + (r + 1) * tr)
            gate = jax.nn.sigmoid(og_ref[rows, :].astype(F32))
            inv_l = 1.0 / acc_ref[rows, FOX_HEAD:FOX_HEAD + 1]
            o_ref[rows, :] = (acc_ref[rows, :FOX_HEAD] * inv_l * gate).astype(o_ref.dtype)


def _fox_attention(qg, k_aug, v_aug, D):
    S = qg.shape[0]
    H = D // FOX_HEAD
    G = FOX_GROUPS
    tk = min(FOX_BLOCK, S // G)
    tq = G * tk
    return pl.pallas_call(
        functools.partial(_fox_body, tk=tk, tr=min(FOX_SUB, tk), G=G),
        grid=(H, S // tq),
        in_specs=[
            pl.BlockSpec((tq, FOX_HEAD), lambda h, i: (i, h)),
            pl.BlockSpec((S, 2 * LANES), lambda h, i: (0, h)),
            pl.BlockSpec((S, 2 * LANES), lambda h, i: (0, h)),
            pl.BlockSpec((tq, FOX_HEAD), lambda h, i: (i, H + h)),
        ],
        out_specs=pl.BlockSpec((tq, FOX_HEAD), lambda h, i: (i, h)),
        out_shape=jax.ShapeDtypeStruct((S, D), BF16),
        scratch_shapes=[pltpu.VMEM((tq, 2 * LANES), BF16), pltpu.VMEM((tk, tk), F32), pltpu.VMEM((tk, tk), F32),
                        pltpu.VMEM((tk, 1), F32), pltpu.VMEM((tk, 1), F32),
                        pltpu.VMEM((tq, 1), F32), pltpu.VMEM((tq, 2 * FOX_HEAD), F32)],
        compiler_params=_params("arbitrary", "arbitrary"),
        name="fox_attention",
    )(qg, k_aug, v_aug, qg)


def _pad_cols(w, n):
    return jnp.pad(w, ((0, 0), (0, n - w.shape[1])))


def _mlp(x, gain, shift, scale, gate, w_up, w_down, layer):
    h = _norm_mod(x, (gain, shift, scale))
    u = _matmul(h, _Weight(w_up, layer), out_dtype=BF16, epilogue=_ep_sqrelu, name=f"mlp_up_l{layer}")
    return _matmul_resid(u, _Weight(w_down, layer), x, gate, name=f"mlp_down_l{layer}")


def kernel(x, c, ada_w, ada_b, norm_mix, norm_mlp, w_mlp_up, w_mlp_down, gla_w_in, gla_w_alpha_up,
           gla_b_alpha, gla_o_norm, gla_w_out, kv_ada_w, kv_ada_b, kv_norm, kv_w, fox_b_f, fox_k_norm,
           fox_w_in, fox_q_norm, fox_w_out):
    B, S, D = x.shape
    assert B == 1 and ada_w.shape[0] == 2
    xs = x[0]
    c_col = c.reshape(D, 1)
    KW = (D // GLA_DV) * GLA_DK

    sh1, sc1, g1, sh2, sc2, g2 = jnp.split(_ada(c_col, ada_w, ada_b[0][None], 0), 6, axis=1)
    h = _norm_mod(xs, (norm_mix[0][None], sh1, sc1))
    n_main = 2 * KW + 2 * D
    proj = _matmul(h, _Weight(gla_w_in[0, :, :n_main].astype(BF16)), out_dtype=BF16, name="gla_in")
    a_lo = _matmul(h, _Weight(_pad_cols(gla_w_in[0, :, n_main:], LANES)), out_dtype=F32, name="gla_in_lowrank")
    w_up = jnp.pad(gla_w_alpha_up[0], ((0, LANES - gla_w_alpha_up.shape[1]), (0, 0))).astype(BF16)
    y, (w_up_b, w_down_b, w_gout_b, w_fout_b) = _gla(
        proj, a_lo, w_up, gla_b_alpha[0][None], gla_o_norm[0][None], D,
        cast_srcs=(w_mlp_up.reshape(-1, w_mlp_up.shape[-1]), w_mlp_down.reshape(-1, w_mlp_down.shape[-1]),
                   gla_w_out[0], fox_w_out[0]))
    w_up_b = w_up_b.reshape(w_mlp_up.shape)
    w_down_b = w_down_b.reshape(w_mlp_down.shape)
    xs = _matmul_resid(y, _Weight(w_gout_b), xs, g1, name="gla_out")
    xs = _mlp(xs, norm_mlp[0][None], sh2, sc2, g2, w_up_b, w_down_b, 0)

    kshift, kscale = jnp.split(_ada(c_col, kv_ada_w, kv_ada_b[None], 0), 2, axis=1)
    sh1, sc1, g1, sh2, sc2, g2 = jnp.split(_ada(c_col, ada_w, ada_b[1][None], 1), 6, axis=1)
    hk, h = _norm_mod(xs, (kv_norm[None], kshift, kscale), (norm_mix[1][None], sh1, sc1))
    kv = _matmul(hk, _Weight(kv_w[:, :2 * D].astype(BF16)), out_dtype=BF16, name="kv_proj")
    f_logit = _matmul(hk, _Weight(_pad_cols(kv_w[:, 2 * D:], LANES)), out_dtype=F32, name="kv_forget")
    k_aug, v_aug = _fox_kv_augment(f_logit, _pad_cols(fox_b_f[None], LANES), fox_k_norm[None], kv, D)

    qg = _matmul_headnorm(h, _Weight(fox_w_in, 0), fox_q_norm[0][None], D, FOX_HEAD ** -0.5 * LOG2E,
                          name="fox_in")
    o = _fox_attention(qg, k_aug, v_aug, D)
    xs = _matmul_resid(o, _Weight(w_fout_b), xs, g1, name="fox_out")
    xs = _mlp(xs, norm_mlp[1][None], sh2, sc2, g2, w_up_b, w_down_b, 1)
    return xs[None]
```

```python
import functools
from typing import NamedTuple

import numpy as np
import jax
import jax.numpy as jnp
from jax import lax
from jax.experimental import pallas as pl
from jax.experimental.pallas import tpu as pltpu

EPS = 1e-6
LANES = 128
BF16_SUBLANES = 16
GLA_DK = 256
GLA_DV = 512
GLA_TAU = 16.0
GLA_CHUNK = 128
GLA_HEADS_PER_STEP = 8
FOX_HEAD = 128
FOX_BLOCK = 1024
FOX_GROUPS = 2
FOX_SUB = 512
NEG_BIG = -1e30
VMEM_LIMIT_BYTES = 56 * 1024 * 1024
MM_TILE_K = 2048
MM_FULL_K = 4096
MM_VMEM_BUDGET = 44 * 1024 * 1024
MM_MIN_TILE_N = 256
LOG2E = 1.4426950408889634

F32 = jnp.float32
BF16 = jnp.bfloat16


def _params(*sem):
    return pltpu.CompilerParams(dimension_semantics=sem, vmem_limit_bytes=VMEM_LIMIT_BYTES)


def _log_sigmoid(x):
    return jnp.minimum(x, 0.0) - jnp.log1p(jnp.exp(-jnp.abs(x)))


def _silu(x):
    return x * jax.nn.sigmoid(x)


def _ada_body(c_ref, w_ref, b_ref, o_ref, cb_ref, *, tn):
    @pl.when(pl.program_id(0) == 0)
    def _():
        cc = c_ref[...]
        cb_ref[...] = jnp.broadcast_to(_silu(cc), cb_ref.shape)
    cb = cb_ref[...]
    parts = []
    for a in range(tn // LANES):
        parts.append(jnp.sum(w_ref[:, a * LANES:(a + 1) * LANES] * cb, axis=0, keepdims=True))
    o_ref[...] = jnp.concatenate(parts, axis=1) + b_ref[...]


def _ada(c_col, w, b_row, layer):
    D = c_col.shape[0]
    N = w.shape[-1]
    tn = min(512, N)
    if w.ndim == 3:
        w_spec = pl.BlockSpec((None, D, tn), lambda j: (layer, 0, j))
    else:
        w_spec = pl.BlockSpec((D, tn), lambda j: (0, j))
    return pl.pallas_call(
        functools.partial(_ada_body, tn=tn),
        grid=(N // tn,),
        in_specs=[pl.BlockSpec((D, 1), lambda j: (0, 0)), w_spec,
                  pl.BlockSpec((1, tn), lambda j: (0, j))],
        out_specs=pl.BlockSpec((1, tn), lambda j: (0, j)),
        out_shape=jax.ShapeDtypeStruct((1, N), F32),
        scratch_shapes=[pltpu.VMEM((D, LANES), F32)],
        compiler_params=_params("arbitrary"),
        name="ada_gemv",
    )(c_col, w, b_row)


def _normmod_body(x_ref, *refs):
    n = len(refs) // 4
    x = x_ref[...]
    y = x * lax.rsqrt(jnp.mean(x * x, axis=-1, keepdims=True) + EPS)
    for t in range(n):
        g_ref, sh_ref, sc_ref = refs[3 * t:3 * t + 3]
        o_ref = refs[3 * n + t]
        o_ref[...] = (y * g_ref[...] * (1.0 + sc_ref[...]) + sh_ref[...]).astype(o_ref.dtype)


def _norm_mod(x, *param_sets):
    S, D = x.shape
    tr = min(512, S)
    row = pl.BlockSpec((1, D), lambda i: (0, 0))
    tile = pl.BlockSpec((tr, D), lambda i: (i, 0))
    outs = pl.pallas_call(
        _normmod_body,
        grid=(S // tr,),
        in_specs=[tile] + [row] * (3 * len(param_sets)),
        out_specs=[tile] * len(param_sets),
        out_shape=[jax.ShapeDtypeStruct((S, D), BF16)] * len(param_sets),
        compiler_params=_params("parallel"),
        name="norm_mod",
    )(x, *(p for ps in param_sets for p in ps))
    return outs[0] if len(param_sets) == 1 else outs


def _ep_cast(acc, extra, o_ref):
    o_ref[...] = acc.astype(o_ref.dtype)


def _ep_sqrelu(acc, extra, o_ref):
    r = jnp.maximum(acc, 0.0)
    o_ref[...] = (r * r).astype(o_ref.dtype)


def _ep_resid(acc, extra, o_ref):
    x_ref, g_ref = extra
    o_ref[...] = x_ref[...] + g_ref[...] * acc


def _ep_headnorm(acc, extra, o_ref, *, n_norm_blocks, scale):
    (gain_ref,) = extra
    j = pl.program_id(1)

    @pl.when(j < n_norm_blocks)
    def _():
        gain = gain_ref[...]
        for g in range(acc.shape[1] // FOX_HEAD):
            blk = acc[:, g * FOX_HEAD:(g + 1) * FOX_HEAD]
            ms = jnp.mean(blk * blk, axis=-1, keepdims=True)
            y = blk * lax.rsqrt(ms + EPS) * gain * scale
            o_ref[:, g * FOX_HEAD:(g + 1) * FOX_HEAD] = y.astype(o_ref.dtype)

    @pl.when(j >= n_norm_blocks)
    def _():
        o_ref[...] = acc.astype(o_ref.dtype)


def _mm_body(*refs, nk, n_extra, epilogue, b_dim):
    a_ref, b_ref = refs[0], refs[1]
    extra = refs[2:2 + n_extra]
    o_ref = refs[2 + n_extra]

    def dot():
        return lax.dot_general(a_ref[...], b_ref[...].astype(BF16), (((1,), (b_dim,)), ((), ())),
                               preferred_element_type=F32)

    if nk == 1:
        epilogue(dot(), extra, o_ref)
        return
    acc_ref = refs[3 + n_extra]
    k = pl.program_id(2)

    @pl.when(k == 0)
    def _():
        acc_ref[...] = jnp.zeros_like(acc_ref)

    acc_ref[...] += dot()

    @pl.when(k == nk - 1)
    def _():
        epilogue(acc_ref[...], extra, o_ref)


class _Weight(NamedTuple):
    array: jax.Array
    layer: int | None = None
    transposed: bool = False
    n_cols: int | None = None

    @property
    def shape(self):
        if self.transposed:
            return self.array.shape[1], self.n_cols or self.array.shape[0]
        return self.array.shape[-2:]


def _mm_tiles(M, w, out_dtype, resid=False):
    K, N = w.shape
    tk = K if K <= MM_FULL_K else MM_TILE_K
    nk = K // tk
    out_bytes = jnp.dtype(out_dtype).itemsize
    w_bytes = w.array.dtype.itemsize
    best = None
    for tm in (1024, 512, 256):
        for tn in (1024, 512, 256, LANES):
            if tm > M or M % tm or tn > N or N % tn or (tn < MM_MIN_TILE_N and tn < N):
                continue
            need = 2 * tm * tk * 2 + 2 * tk * tn * w_bytes + 2 * tm * tn * out_bytes
            need += (2 * tm * tn * 4 if resid else 0) + (tm * tn * 4 if nk > 1 else 0)
            if need > MM_VMEM_BUDGET:
                continue
            traffic = M * K * 2 * (N // tn if nk > 1 else 1) + K * N * w_bytes * (M // tm)
            if best is None or (traffic, -tn) < best[0]:
                best = ((traffic, -tn), (tm, tn, tk))
    return best[1]


def _matmul(a, w, *, out_dtype, epilogue=_ep_cast, extras=(), extra_specs=(), tiles=None, name):
    M, K = a.shape
    N = w.shape[1]
    tm, tn, tk = tiles or _mm_tiles(M, w, out_dtype)
    nk = K // tk
    scratch = [pltpu.VMEM((tm, tn), F32)] if nk > 1 else []
    if w.transposed:
        w_spec = pl.BlockSpec((tn, tk), lambda i, j, k: (j, k))
    elif w.array.ndim == 3:
        w_spec = pl.BlockSpec((None, tk, tn), lambda i, j, k: (w.layer, k, j))
    else:
        w_spec = pl.BlockSpec((tk, tn), lambda i, j, k: (k, j))
    return pl.pallas_call(
        functools.partial(_mm_body, nk=nk, n_extra=len(extras), epilogue=epilogue, b_dim=1 if w.transposed else 0),
        grid=(M // tm, N // tn, nk),
        in_specs=[pl.BlockSpec((tm, tk), lambda i, j, k: (i, k)), w_spec, *extra_specs],
        out_specs=pl.BlockSpec((tm, tn), lambda i, j, k: (i, j)),
        out_shape=jax.ShapeDtypeStruct((M, N), out_dtype),
        scratch_shapes=scratch,
        compiler_params=_params("parallel", "parallel", "arbitrary"),
        name=name,
    )(a, w.array, *extras)


def _matmul_resid(a, w, x, gate, name):
    M, N = x.shape
    tm, tn, _ = tiles = _mm_tiles(M, w, F32, resid=True)
    return _matmul(a, w, out_dtype=F32, epilogue=_ep_resid, extras=(x, gate),
                   extra_specs=(pl.BlockSpec((tm, tn), lambda i, j, k: (i, j)),
                                pl.BlockSpec((1, tn), lambda i, j, k: (0, j))), tiles=tiles, name=name)


def _matmul_headnorm(a, w, gain, n_norm_cols, scale, name):
    tiles = _mm_tiles(a.shape[0], w, BF16)
    ep = functools.partial(_ep_headnorm, n_norm_blocks=n_norm_cols // tiles[1], scale=scale)
    return _matmul(a, w, out_dtype=BF16, epilogue=ep, extras=(gain,),
                   extra_specs=(pl.BlockSpec((1, FOX_HEAD), lambda i, j, k: (0, 0)),), tiles=tiles, name=name)


def _gla_prefix_matrix(C):
    L = int(np.log2(C))
    t = np.arange(C)[:, None]
    j = np.arange(C)[None, :]
    mats = []
    for l in range(L):
        n = 1 << l
        mid = ((t >> (l + 1)) << (l + 1)) + n
        upper = ((t >> l) & 1) == 1
        mats.append(np.where(upper, (j >= mid) & (j <= t), (j > t) & (j < mid)))
    mats.append(j <= t)
    mats.append(j > t)
    return np.concatenate(mats, axis=0).astype(np.float32)


def _gla_body(*refs, C, L, HP, n_cast):
    q_ref, k_ref, v_ref, r_ref, alo_ref, wup_ref, balpha_ref, onorm_ref, f_ref = refs[:9]
    cast_in = refs[9:9 + n_cast]
    o_ref = refs[9 + n_cast]
    cast_out = refs[10 + n_cast:10 + 2 * n_cast]
    state_ref = refs[10 + 2 * n_cast]
    c = pl.program_id(0)
    hb = pl.program_id(1)

    for src, dst in zip(cast_in, cast_out):
        dst[...] = src[...].astype(BF16)

    row = lax.broadcasted_iota(jnp.int32, (C, C), 0)
    col = lax.broadcasted_iota(jnp.int32, (C, C), 1)
    a_lo = alo_ref[...].astype(BF16)
    fmat = f_ref[...]

    heads = range(HP)
    kcols = [slice(j * GLA_DK, (j + 1) * GLA_DK) for j in heads]
    vcols = [slice(j * GLA_DV, (j + 1) * GLA_DV) for j in heads]

    @pl.when(c == 0)
    def _():
        for j in heads:
            state_ref[hb * HP + j] = jnp.zeros(state_ref.shape[1:], F32)

    st = [state_ref[hb * HP + j] for j in heads]
    q = [q_ref[:, kcols[j]].astype(F32) * (GLA_DK ** -0.5) for j in heads]
    k = [k_ref[:, kcols[j]].astype(F32) for j in heads]
    v = [v_ref[:, vcols[j]] for j in heads]

    xs = []
    for j in heads:
        z = jnp.dot(a_lo, wup_ref[:, kcols[j]], preferred_element_type=F32) + balpha_ref[:, kcols[j]]
        la = _log_sigmoid(z) * (1.0 / GLA_TAU)
        hi = la.astype(BF16)
        mid = (la - hi.astype(F32)).astype(BF16)
        x = jnp.dot(fmat, jnp.concatenate([hi, mid], axis=1), preferred_element_type=F32)
        xs.append(x[:, :GLA_DK] + x[:, GLA_DK:])

    scores = [jnp.where(row == col, jnp.sum(q[j] * k[j], axis=-1, keepdims=True), 0.0) for j in heads]
    for l in range(L):
        rl = row >> l
        cl = col >> l
        pick = ((rl ^ cl) == 1) & (rl > cl)
        for j in heads:
            e = jnp.exp(xs[j][l * C:(l + 1) * C])
            s_l = lax.dot_general((q[j] * e).astype(BF16), (k[j] * e).astype(BF16),
                                  (((1,), (1,)), ((), ())), preferred_element_type=F32)
            scores[j] = jnp.where(pick, s_l, scores[j])

    outs = []
    for j in heads:
        b = xs[j][L * C:(L + 1) * C]
        rem = xs[j][(L + 1) * C:]
        o = jnp.dot(scores[j].astype(BF16), v[j], preferred_element_type=F32)
        o += jnp.dot((q[j] * jnp.exp(b)).astype(BF16), st[j].astype(BF16), preferred_element_type=F32)
        outs.append(o)
        kt = jnp.transpose(k[j] * jnp.exp(rem)).astype(BF16)
        d_row = jnp.exp(b[C - 1:C, :])
        d_col = jnp.transpose(jnp.broadcast_to(d_row, (LANES, GLA_DK)))[:, :1]
        st[j] = d_col * st[j] + jnp.dot(kt, v[j], preferred_element_type=F32)

    for j in heads:
        state_ref[hb * HP + j] = st[j]
        o = outs[j]
        ms = jnp.mean(o * o, axis=-1, keepdims=True)
        y = o * lax.rsqrt(ms + EPS) * onorm_ref[...]
        o_ref[:, vcols[j]] = (y * _silu(r_ref[:, vcols[j]].astype(F32))).astype(o_ref.dtype)


def _gla(proj, a_lo, w_up, b_alpha, o_norm, D, cast_srcs=()):
    S = proj.shape[0]
    H = D // GLA_DV
    KW = H * GLA_DK
    C = min(GLA_CHUNK, S)
    NC = S // C
    L = int(np.log2(C))
    HP = min(GLA_HEADS_PER_STEP, H)
    fmat = jnp.asarray(_gla_prefix_matrix(C), BF16)
    wk, wv = HP * GLA_DK, HP * GLA_DV
    kq, kv, kr = KW // wk, (2 * KW) // wv, (2 * KW + D) // wv
    cast_specs = []
    for w in cast_srcs:
        rows = w.shape[0] // NC
        assert rows * NC == w.shape[0] and rows % BF16_SUBLANES == 0, w.shape
        cast_specs.append(pl.BlockSpec((rows, w.shape[1]), lambda c, h: (c, 0)))
    outs = pl.pallas_call(
        functools.partial(_gla_body, C=C, L=L, HP=HP, n_cast=len(cast_srcs)),
        grid=(NC, H // HP),
        in_specs=[
            pl.BlockSpec((C, wk), lambda c, h: (c, h)),
            pl.BlockSpec((C, wk), lambda c, h: (c, kq + h)),
            pl.BlockSpec((C, wv), lambda c, h: (c, kv + h)),
            pl.BlockSpec((C, wv), lambda c, h: (c, kr + h)),
            pl.BlockSpec((C, LANES), lambda c, h: (c, 0)),
            pl.BlockSpec((LANES, wk), lambda c, h: (0, h)),
            pl.BlockSpec((1, wk), lambda c, h: (0, h)),
            pl.BlockSpec((1, GLA_DV), lambda c, h: (0, 0)),
            pl.BlockSpec(fmat.shape, lambda c, h: (0, 0)),
            *cast_specs,
        ],
        out_specs=[pl.BlockSpec((C, wv), lambda c, h: (c, h)), *cast_specs],
        out_shape=[jax.ShapeDtypeStruct((S, D), BF16), *(jax.ShapeDtypeStruct(w.shape, BF16) for w in cast_srcs)],
        scratch_shapes=[pltpu.VMEM((H, GLA_DK, GLA_DV), F32)],
        compiler_params=_params("arbitrary", "arbitrary"),
        name="gla_chunk",
    )(proj, proj, proj, proj, a_lo, w_up, b_alpha, o_norm, fmat, *cast_srcs)
    return outs[0], outs[1:]


def _split3(x):
    hi = x.astype(BF16)
    r1 = x - hi.astype(F32)
    mid = r1.astype(BF16)
    lo = (r1 - mid.astype(F32)).astype(BF16)
    return hi, mid, lo


def _kvaug_body(f_ref, bf_ref, tri_ref, kn_ref, k_ref, v_ref, ka_ref, va_ref, carry_ref, *, T, H):
    @pl.when(pl.program_id(0) == 0)
    def _():
        carry_ref[...] = jnp.zeros_like(carry_ref)

    lf = _log_sigmoid(f_ref[...] + bf_ref[...])
    cs = jnp.dot(tri_ref[...], jnp.concatenate(_split3(lf), axis=1), preferred_element_type=F32)
    cum = cs[:, :LANES] + cs[:, LANES:2 * LANES] + cs[:, 2 * LANES:] + carry_ref[...]
    carry_ref[...] = cum[T - 1:T, :]
    lane = lax.broadcasted_iota(jnp.int32, (T, LANES), 1)
    v_tail = jnp.where(lane == 0, 1.0, 0.0).astype(BF16)
    for h in range(H):
        val = cum[:, h:h + 1] * (-LOG2E)
        hi = val.astype(BF16).astype(F32)
        mid = (val - hi).astype(BF16).astype(F32)
        lo = val - hi - mid
        k_tail = jnp.where(lane == 0, hi, jnp.where(lane == 1, mid, jnp.where(lane == 2, lo, 0.0)))
        head = slice(h * FOX_HEAD, (h + 1) * FOX_HEAD)
        kh = k_ref[:, head].astype(F32)
        kh = kh * lax.rsqrt(jnp.mean(kh * kh, axis=-1, keepdims=True) + EPS) * kn_ref[...]
        ka_ref[:, 2 * h * LANES:(2 * h + 1) * LANES] = kh.astype(BF16)
        ka_ref[:, (2 * h + 1) * LANES:(2 * h + 2) * LANES] = k_tail.astype(BF16)
        va_ref[:, 2 * h * LANES:(2 * h + 1) * LANES] = v_ref[:, head]
        va_ref[:, (2 * h + 1) * LANES:(2 * h + 2) * LANES] = v_tail


def _fox_kv_augment(f_logit, b_f, k_gain, kv, D):
    S = f_logit.shape[0]
    H = D // FOX_HEAD
    T = min(256, S)
    tri = jnp.asarray(np.tril(np.ones((T, T), np.float32)), BF16)
    aug = jax.ShapeDtypeStruct((S, 2 * D), BF16)
    return pl.pallas_call(
        functools.partial(_kvaug_body, T=T, H=H),
        grid=(S // T,),
        in_specs=[pl.BlockSpec((T, LANES), lambda i: (i, 0)),
                  pl.BlockSpec((1, LANES), lambda i: (0, 0)),
                  pl.BlockSpec((T, T), lambda i: (0, 0)),
                  pl.BlockSpec((1, FOX_HEAD), lambda i: (0, 0)),
                  pl.BlockSpec((T, D), lambda i: (i, 0)),
                  pl.BlockSpec((T, D), lambda i: (i, 1))],
        out_specs=[pl.BlockSpec((T, 2 * D), lambda i: (i, 0)), pl.BlockSpec((T, 2 * D), lambda i: (i, 0))],
        out_shape=[aug, aug],
        scratch_shapes=[pltpu.VMEM((1, LANES), F32)],
        compiler_params=_params("arbitrary"),
        name="fox_kv_augment",
    )(f_logit, b_f, tri, k_gain, kv, kv)


def _fox_body(q_ref, k_ref, v_ref, og_ref, o_ref, qa_ref, sa_ref, sb_ref, xa_ref, xb_ref, m_ref, acc_ref,
              *, tk, tr, G):
    i = pl.program_id(1)
    nsub = tk // tr
    bufs = ((sa_ref, xa_ref), (sb_ref, xb_ref))
    lane_q = lax.broadcasted_iota(jnp.int32, (G * tk, LANES), 1)
    qa_ref[:, :FOX_HEAD] = q_ref[...]
    qa_ref[:, FOX_HEAD:] = jnp.where(lane_q < 3, 1.0, 0.0).astype(BF16)
    m_ref[...] = jnp.full(m_ref.shape, NEG_BIG, F32)
    acc_ref[...] = jnp.zeros(acc_ref.shape, F32)

    def logits(g, cidx, buf):
        s_ref, x_ref = bufs[buf]
        kc = k_ref[pl.ds(pl.multiple_of(cidx * tk, tk), tk), :]
        for r in range(nsub):
            rows = slice(r * tr, (r + 1) * tr)
            s = lax.dot_general(qa_ref[g * tk + r * tr:g * tk + (r + 1) * tr, :], kc, (((1,), (1,)), ((), ())),
                                preferred_element_type=F32)
            s_ref[rows, :] = s
            x_ref[rows, :] = jnp.max(s, axis=-1, keepdims=True)

    def softmax_pv(g, cidx, buf, masked):
        s_ref, x_ref = bufs[buf]
        c0 = pl.multiple_of(cidx * tk, tk)
        for r in range(nsub):
            rows = slice(r * tr, (r + 1) * tr)
            grows = slice(g * tk + r * tr, g * tk + (r + 1) * tr)
            if masked:
                ncol = (r + 1) * tr
                row = lax.broadcasted_iota(jnp.int32, (tr, ncol), 0) + r * tr
                col = lax.broadcasted_iota(jnp.int32, (tr, ncol), 1)
                s = jnp.where(row >= col, s_ref[rows, :ncol], NEG_BIG)
                s_max = jnp.max(s, axis=-1, keepdims=True)
            else:
                ncol = tk
                s = s_ref[rows, :]
                s_max = x_ref[rows, :]
            vc = v_ref[pl.ds(c0, ncol), :]
            m_old = m_ref[grows, :]
            m_new = jnp.maximum(m_old, s_max)
            m_ref[grows, :] = m_new
            p = jnp.exp2(s - m_new).astype(BF16)
            acc_ref[grows, :] = (jnp.exp2(m_old - m_new) * acc_ref[grows, :]
                                 + jnp.dot(p, vc, preferred_element_type=F32))

    def full_blocks(cidx, carry):
        for g in range(G):
            if g + 1 < G:
                logits(g + 1, cidx, (g + 1) % 2)
            else:
                logits(0, cidx + 1, 0)
            softmax_pv(g, cidx, g % 2, False)
        return carry

    logits(0, 0, 0)
    lax.fori_loop(0, G * i, full_blocks, 0)
    tail = [(g, G * i + d, g == d) for d in range(G) for g in range(d, G)]
    for t, (g, cidx, masked) in enumerate(tail):
        if t + 1 < len(tail):
            logits(tail[t + 1][0], tail[t + 1][1], (t + 1) % 2)
        softmax_pv(g, cidx, t % 2, masked)

    for g in range(G):
        for r in range(nsub):
            rows = slice(g * tk + r * tr, g * tk + (r + 1) * tr)
            gate = jax.nn.sigmoid(og_ref[rows, :].astype(F32))
            inv_l = 1.0 / acc_ref[rows, FOX_HEAD:FOX_HEAD + 1]
            o_ref[rows, :] = (acc_ref[rows, :FOX_HEAD] * inv_l * gate).astype(o_ref.dtype)


def _fox_attention(qg, k_aug, v_aug, D):
    S = qg.shape[0]
    H = D // FOX_HEAD
    G = FOX_GROUPS
    tk = min(FOX_BLOCK, S // G)
    tq = G * tk
    return pl.pallas_call(
        functools.partial(_fox_body, tk=tk, tr=min(FOX_SUB, tk), G=G),
        grid=(H, S // tq),
        in_specs=[
            pl.BlockSpec((tq, FOX_HEAD), lambda h, i: (i, h)),
            pl.BlockSpec((S, 2 * LANES), lambda h, i: (0, h)),
            pl.BlockSpec((S, 2 * LANES), lambda h, i: (0, h)),
            pl.BlockSpec((tq, FOX_HEAD), lambda h, i: (i, H + h)),
        ],
        out_specs=pl.BlockSpec((tq, FOX_HEAD), lambda h, i: (i, h)),
        out_shape=jax.ShapeDtypeStruct((S, D), BF16),
        scratch_shapes=[pltpu.VMEM((tq, 2 * LANES), BF16), pltpu.VMEM((tk, tk), F32), pltpu.VMEM((tk, tk), F32),
                        pltpu.VMEM((tk, 1), F32), pltpu.VMEM((tk, 1), F32),
                        pltpu.VMEM((tq, 1), F32), pltpu.VMEM((tq, 2 * FOX_HEAD), F32)],
        compiler_params=_params("arbitrary", "arbitrary"),
        name="fox_attention",
    )(qg, k_aug, v_aug, qg)


def _pad_cols(w, n):
    return jnp.pad(w, ((0, 0), (0, n - w.shape[1])))


def _pad_rows(w, n):
    return jnp.pad(w, ((0, n - w.shape[0]), (0, 0)))


def _mlp(x, gain, shift, scale, gate, w_up, w_down, layer):
    h = _norm_mod(x, (gain, shift, scale))
    u = _matmul(h, _Weight(w_up, layer), out_dtype=BF16, epilogue=_ep_sqrelu, name=f"mlp_up_l{layer}")
    return _matmul_resid(u, _Weight(w_down, layer), x, gate, name=f"mlp_down_l{layer}")


def kernel(x, c, ada_w, ada_b, norm_mix, norm_mlp, w_mlp_up, w_mlp_down, gla_w_in, gla_w_alpha_up,
           gla_b_alpha, gla_o_norm, gla_w_out, kv_ada_w, kv_ada_b, kv_norm, kv_w, fox_b_f, fox_k_norm,
           fox_w_in, fox_q_norm, fox_w_out):
    B, S, D = x.shape
    assert B == 1 and ada_w.shape[0] == 2
    xs = x[0]
    c_col = c.reshape(D, 1)
    KW = (D // GLA_DV) * GLA_DK

    sh1, sc1, g1, sh2, sc2, g2 = jnp.split(_ada(c_col, ada_w, ada_b[0][None], 0), 6, axis=1)
    h = _norm_mod(xs, (norm_mix[0][None], sh1, sc1))
    n_main = 2 * KW + 2 * D
    w_in_t = jnp.swapaxes(gla_w_in, 1, 2)[0].astype(BF16)
    proj = _matmul(h, _Weight(w_in_t, transposed=True, n_cols=n_main), out_dtype=BF16, name="gla_in")
    a_lo = _matmul(h, _Weight(_pad_rows(w_in_t[n_main:], LANES), transposed=True), out_dtype=F32,
                   name="gla_in_lowrank")
    w_up = jnp.pad(gla_w_alpha_up[0], ((0, LANES - gla_w_alpha_up.shape[1]), (0, 0))).astype(BF16)
    y, (w_up_b, w_down_b, w_gout_b, w_fout_b) = _gla(
        proj, a_lo, w_up, gla_b_alpha[0][None], gla_o_norm[0][None], D,
        cast_srcs=(w_mlp_up.reshape(-1, w_mlp_up.shape[-1]), w_mlp_down.reshape(-1, w_mlp_down.shape[-1]),
                   gla_w_out[0], fox_w_out[0]))
    w_up_b = w_up_b.reshape(w_mlp_up.shape)
    w_down_b = w_down_b.reshape(w_mlp_down.shape)
    xs = _matmul_resid(y, _Weight(w_gout_b), xs, g1, name="gla_out")
    xs = _mlp(xs, norm_mlp[0][None], sh2, sc2, g2, w_up_b, w_down_b, 0)

    kshift, kscale = jnp.split(_ada(c_col, kv_ada_w, kv_ada_b[None], 0), 2, axis=1)
    sh1, sc1, g1, sh2, sc2, g2 = jnp.split(_ada(c_col, ada_w, ada_b[1][None], 1), 6, axis=1)
    hk, h = _norm_mod(xs, (kv_norm[None], kshift, kscale), (norm_mix[1][None], sh1, sc1))
    kv_w_t = kv_w.T.astype(BF16)
    kv = _matmul(hk, _Weight(kv_w_t, transposed=True, n_cols=2 * D), out_dtype=BF16, name="kv_proj")
    f_logit = _matmul(hk, _Weight(_pad_rows(kv_w_t[2 * D:], LANES), transposed=True), out_dtype=F32,
                      name="kv_forget")
    k_aug, v_aug = _fox_kv_augment(f_logit, _pad_cols(fox_b_f[None], LANES), fox_k_norm[None], kv, D)

    qg = _matmul_headnorm(h, _Weight(fox_w_in, 0), fox_q_norm[0][None], D, FOX_HEAD ** -0.5 * LOG2E,
                          name="fox_in")
    o = _fox_attention(qg, k_aug, v_aug, D)
    xs = _matmul_resid(o, _Weight(w_fout_b), xs, g1, name="fox_out")
    xs = _mlp(xs, norm_mlp[1][None], sh2, sc2, g2, w_up_b, w_down_b, 1)
    return xs[None]
```

```python
import functools
from typing import NamedTuple

import numpy as np
import jax
import jax.numpy as jnp
from jax import lax
from jax.experimental import pallas as pl
from jax.experimental.pallas import tpu as pltpu

EPS = 1e-6
LANES = 128
BF16_SUBLANES = 16
GLA_DK = 256
GLA_DV = 512
GLA_TAU = 16.0
GLA_CHUNK = 128
GLA_HEADS_PER_STEP = 8
FOX_HEAD = 128
FOX_BLOCK = 1024
FOX_GROUPS = 2
FOX_SUB = 512
NEG_BIG = -1e30
VMEM_LIMIT_BYTES = 56 * 1024 * 1024
MM_TILE_K = 2048
MM_FULL_K = 4096
MM_VMEM_BUDGET = 44 * 1024 * 1024
MM_MIN_TILE_N = 256
LOG2E = 1.4426950408889634

F32 = jnp.float32
BF16 = jnp.bfloat16


def _params(*sem):
    return pltpu.CompilerParams(dimension_semantics=sem, vmem_limit_bytes=VMEM_LIMIT_BYTES)


def _log_sigmoid(x):
    return jnp.minimum(x, 0.0) - jnp.log(1.0 + jnp.exp(-jnp.abs(x)))


def _silu(x):
    return x * jax.nn.sigmoid(x)


def _ada_body(c_ref, w_ref, b_ref, o_ref, cb_ref, *, tn):
    @pl.when(pl.program_id(0) == 0)
    def _():
        cc = c_ref[...]
        cb_ref[...] = jnp.broadcast_to(_silu(cc), cb_ref.shape)
    cb = cb_ref[...]
    parts = []
    for a in range(tn // LANES):
        parts.append(jnp.sum(w_ref[:, a * LANES:(a + 1) * LANES] * cb, axis=0, keepdims=True))
    o_ref[...] = jnp.concatenate(parts, axis=1) + b_ref[...]


def _ada(c_col, w, b_row, layer):
    D = c_col.shape[0]
    N = w.shape[-1]
    tn = min(512, N)
    if w.ndim == 3:
        w_spec = pl.BlockSpec((None, D, tn), lambda j: (layer, 0, j))
    else:
        w_spec = pl.BlockSpec((D, tn), lambda j: (0, j))
    return pl.pallas_call(
        functools.partial(_ada_body, tn=tn),
        grid=(N // tn,),
        in_specs=[pl.BlockSpec((D, 1), lambda j: (0, 0)), w_spec,
                  pl.BlockSpec((1, tn), lambda j: (0, j))],
        out_specs=pl.BlockSpec((1, tn), lambda j: (0, j)),
        out_shape=jax.ShapeDtypeStruct((1, N), F32),
        scratch_shapes=[pltpu.VMEM((D, LANES), F32)],
        compiler_params=_params("arbitrary"),
        name="ada_gemv",
    )(c_col, w, b_row)


def _normmod_body(x_ref, *refs):
    n = len(refs) // 4
    x = x_ref[...]
    y = x * lax.rsqrt(jnp.mean(x * x, axis=-1, keepdims=True) + EPS)
    for t in range(n):
        g_ref, sh_ref, sc_ref = refs[3 * t:3 * t + 3]
        o_ref = refs[3 * n + t]
        o_ref[...] = (y * g_ref[...] * (1.0 + sc_ref[...]) + sh_ref[...]).astype(o_ref.dtype)


def _norm_mod(x, *param_sets):
    S, D = x.shape
    tr = min(512, S)
    row = pl.BlockSpec((1, D), lambda i: (0, 0))
    tile = pl.BlockSpec((tr, D), lambda i: (i, 0))
    outs = pl.pallas_call(
        _normmod_body,
        grid=(S // tr,),
        in_specs=[tile] + [row] * (3 * len(param_sets)),
        out_specs=[tile] * len(param_sets),
        out_shape=[jax.ShapeDtypeStruct((S, D), BF16)] * len(param_sets),
        compiler_params=_params("parallel"),
        name="norm_mod",
    )(x, *(p for ps in param_sets for p in ps))
    return outs[0] if len(param_sets) == 1 else outs


def _ep_cast(acc, extra, o_ref):
    o_ref[...] = acc.astype(o_ref.dtype)


def _ep_sqrelu(acc, extra, o_ref):
    r = jnp.maximum(acc, 0.0)
    o_ref[...] = (r * r).astype(o_ref.dtype)


def _ep_resid(acc, extra, o_ref):
    x_ref, g_ref = extra
    o_ref[...] = x_ref[...] + g_ref[...] * acc


def _ep_headnorm(acc, extra, o_ref, *, n_norm_blocks, scale):
    (gain_ref,) = extra
    j = pl.program_id(1)

    @pl.when(j < n_norm_blocks)
    def _():
        gain = gain_ref[...]
        for g in range(acc.shape[1] // FOX_HEAD):
            blk = acc[:, g * FOX_HEAD:(g + 1) * FOX_HEAD]
            ms = jnp.mean(blk * blk, axis=-1, keepdims=True)
            y = blk * lax.rsqrt(ms + EPS) * gain * scale
            o_ref[:, g * FOX_HEAD:(g + 1) * FOX_HEAD] = y.astype(o_ref.dtype)

    @pl.when(j >= n_norm_blocks)
    def _():
        o_ref[...] = acc.astype(o_ref.dtype)


def _mm_body(*refs, nk, n_extra, epilogue, b_dim):
    a_ref, b_ref = refs[0], refs[1]
    extra = refs[2:2 + n_extra]
    o_ref = refs[2 + n_extra]

    def dot():
        return lax.dot_general(a_ref[...], b_ref[...].astype(BF16), (((1,), (b_dim,)), ((), ())),
                               preferred_element_type=F32)

    if nk == 1:
        epilogue(dot(), extra, o_ref)
        return
    acc_ref = refs[3 + n_extra]
    k = pl.program_id(2)

    @pl.when(k == 0)
    def _():
        acc_ref[...] = jnp.zeros_like(acc_ref)

    acc_ref[...] += dot()

    @pl.when(k == nk - 1)
    def _():
        epilogue(acc_ref[...], extra, o_ref)


class _Weight(NamedTuple):
    array: jax.Array
    layer: int | None = None
    transposed: bool = False
    n_cols: int | None = None

    @property
    def shape(self):
        if self.transposed:
            return self.array.shape[1], self.n_cols or self.array.shape[0]
        return self.array.shape[-2:]


def _mm_tiles(M, w, out_dtype, resid=False):
    K, N = w.shape
    tk = K if K <= MM_FULL_K else MM_TILE_K
    nk = K // tk
    out_bytes = jnp.dtype(out_dtype).itemsize
    w_bytes = w.array.dtype.itemsize
    best = None
    for tm in (1024, 512, 256):
        for tn in (1024, 512, 256, LANES):
            if tm > M or M % tm or tn > N or N % tn or (tn < MM_MIN_TILE_N and tn < N):
                continue
            need = 2 * tm * tk * 2 + 2 * tk * tn * w_bytes + 2 * tm * tn * out_bytes
            need += (2 * tm * tn * 4 if resid else 0) + (tm * tn * 4 if nk > 1 else 0)
            if need > MM_VMEM_BUDGET:
                continue
            traffic = M * K * 2 * (N // tn if nk > 1 else 1) + K * N * w_bytes * (M // tm)
            if best is None or (traffic, -tn) < best[0]:
                best = ((traffic, -tn), (tm, tn, tk))
    return best[1]


def _matmul(a, w, *, out_dtype, epilogue=_ep_cast, extras=(), extra_specs=(), tiles=None, name):
    M, K = a.shape
    N = w.shape[1]
    tm, tn, tk = tiles or _mm_tiles(M, w, out_dtype)
    nk = K // tk
    scratch = [pltpu.VMEM((tm, tn), F32)] if nk > 1 else []
    if w.transposed:
        w_spec = pl.BlockSpec((tn, tk), lambda i, j, k: (j, k))
    elif w.array.ndim == 3:
        w_spec = pl.BlockSpec((None, tk, tn), lambda i, j, k: (w.layer, k, j))
    else:
        w_spec = pl.BlockSpec((tk, tn), lambda i, j, k: (k, j))
    return pl.pallas_call(
        functools.partial(_mm_body, nk=nk, n_extra=len(extras), epilogue=epilogue, b_dim=1 if w.transposed else 0),
        grid=(M // tm, N // tn, nk),
        in_specs=[pl.BlockSpec((tm, tk), lambda i, j, k: (i, k)), w_spec, *extra_specs],
        out_specs=pl.BlockSpec((tm, tn), lambda i, j, k: (i, j)),
        out_shape=jax.ShapeDtypeStruct((M, N), out_dtype),
        scratch_shapes=scratch,
        compiler_params=_params("parallel", "parallel", "arbitrary"),
        name=name,
    )(a, w.array, *extras)


def _matmul_resid(a, w, x, gate, name):
    M, N = x.shape
    tm, tn, _ = tiles = _mm_tiles(M, w, F32, resid=True)
    return _matmul(a, w, out_dtype=F32, epilogue=_ep_resid, extras=(x, gate),
                   extra_specs=(pl.BlockSpec((tm, tn), lambda i, j, k: (i, j)),
                                pl.BlockSpec((1, tn), lambda i, j, k: (0, j))), tiles=tiles, name=name)


def _matmul_headnorm(a, w, gain, n_norm_cols, scale, name):
    tiles = _mm_tiles(a.shape[0], w, BF16)
    ep = functools.partial(_ep_headnorm, n_norm_blocks=n_norm_cols // tiles[1], scale=scale)
    return _matmul(a, w, out_dtype=BF16, epilogue=ep, extras=(gain,),
                   extra_specs=(pl.BlockSpec((1, FOX_HEAD), lambda i, j, k: (0, 0)),), tiles=tiles, name=name)


def _gla_prefix_matrix(C):
    L = int(np.log2(C))
    t = np.arange(C)[:, None]
    j = np.arange(C)[None, :]
    mats = []
    for l in range(L):
        n = 1 << l
        mid = ((t >> (l + 1)) << (l + 1)) + n
        upper = ((t >> l) & 1) == 1
        mats.append(np.where(upper, (j >= mid) & (j <= t), (j > t) & (j < mid)))
    mats.append(j <= t)
    mats.append(j > t)
    return np.concatenate(mats, axis=0).astype(np.float32)


def _gla_body(*refs, C, L, HP, n_cast):
    q_ref, k_ref, v_ref, r_ref, alo_ref, wup_ref, balpha_ref, onorm_ref, f_ref = refs[:9]
    cast_in = refs[9:9 + n_cast]
    o_ref = refs[9 + n_cast]
    cast_out = refs[10 + n_cast:10 + 2 * n_cast]
    state_ref = refs[10 + 2 * n_cast]
    c = pl.program_id(0)
    hb = pl.program_id(1)

    for src, dst in zip(cast_in, cast_out):
        dst[...] = src[...].astype(BF16)

    row = lax.broadcasted_iota(jnp.int32, (C, C), 0)
    col = lax.broadcasted_iota(jnp.int32, (C, C), 1)
    a_lo = alo_ref[...].astype(BF16)
    fmat = f_ref[...]

    heads = range(HP)
    kcols = [slice(j * GLA_DK, (j + 1) * GLA_DK) for j in heads]
    vcols = [slice(j * GLA_DV, (j + 1) * GLA_DV) for j in heads]

    @pl.when(c == 0)
    def _():
        for j in heads:
            state_ref[hb * HP + j] = jnp.zeros(state_ref.shape[1:], F32)

    st = [state_ref[hb * HP + j] for j in heads]
    q = [q_ref[:, kcols[j]].astype(F32) * (GLA_DK ** -0.5) for j in heads]
    k = [k_ref[:, kcols[j]].astype(F32) for j in heads]
    qb = [q_ref[:, kcols[j]] * jnp.asarray(GLA_DK ** -0.5, BF16) for j in heads]
    kb = [k_ref[:, kcols[j]] for j in heads]
    v = [v_ref[:, vcols[j]] for j in heads]

    xs = []
    for j in heads:
        z = jnp.dot(a_lo, wup_ref[:, kcols[j]], preferred_element_type=F32) + balpha_ref[:, kcols[j]]
        la = _log_sigmoid(z) * (1.0 / GLA_TAU)
        hi = la.astype(BF16)
        mid = (la - hi.astype(F32)).astype(BF16)
        xs.append(jnp.dot(fmat, jnp.concatenate([hi, mid], axis=0), preferred_element_type=F32))

    scores = [jnp.where(row == col, jnp.sum(q[j] * k[j], axis=-1, keepdims=True), 0.0) for j in heads]
    for l in range(L):
        rl = row >> l
        cl = col >> l
        pick = ((rl ^ cl) == 1) & (rl > cl)
        for j in heads:
            e = jnp.exp(xs[j][l * C:(l + 1) * C]).astype(BF16)
            s_l = lax.dot_general(qb[j] * e, kb[j] * e, (((1,), (1,)), ((), ())), preferred_element_type=F32)
            scores[j] = jnp.where(pick, s_l, scores[j])

    outs = []
    for j in heads:
        b = xs[j][L * C:(L + 1) * C]
        rem = xs[j][(L + 1) * C:]
        o = jnp.dot(scores[j].astype(BF16), v[j], preferred_element_type=F32)
        o += jnp.dot((q[j] * jnp.exp(b)).astype(BF16), st[j].astype(BF16), preferred_element_type=F32)
        outs.append(o)
        kt = jnp.transpose(k[j] * jnp.exp(rem)).astype(BF16)
        d_row = jnp.exp(b[C - 1:C, :])
        d_col = jnp.transpose(jnp.broadcast_to(d_row, (LANES, GLA_DK)))[:, :1]
        st[j] = d_col * st[j] + jnp.dot(kt, v[j], preferred_element_type=F32)

    for j in heads:
        state_ref[hb * HP + j] = st[j]
        o = outs[j]
        ms = jnp.mean(o * o, axis=-1, keepdims=True)
        y = o * lax.rsqrt(ms + EPS) * onorm_ref[...]
        o_ref[:, vcols[j]] = (y * _silu(r_ref[:, vcols[j]].astype(F32))).astype(o_ref.dtype)


def _gla(proj, a_lo, w_up, b_alpha, o_norm, D, cast_srcs=()):
    S = proj.shape[0]
    H = D // GLA_DV
    KW = H * GLA_DK
    C = min(GLA_CHUNK, S)
    NC = S // C
    L = int(np.log2(C))
    HP = min(GLA_HEADS_PER_STEP, H)
    fmat = jnp.asarray(np.tile(_gla_prefix_matrix(C), (1, 2)), BF16)
    wk, wv = HP * GLA_DK, HP * GLA_DV
    kq, kv, kr = KW // wk, (2 * KW) // wv, (2 * KW + D) // wv
    cast_specs = []
    for w in cast_srcs:
        rows = w.shape[0] // NC
        assert rows * NC == w.shape[0] and rows % BF16_SUBLANES == 0, w.shape
        cast_specs.append(pl.BlockSpec((rows, w.shape[1]), lambda c, h: (c, 0)))
    outs = pl.pallas_call(
        functools.partial(_gla_body, C=C, L=L, HP=HP, n_cast=len(cast_srcs)),
        grid=(NC, H // HP),
        in_specs=[
            pl.BlockSpec((C, wk), lambda c, h: (c, h)),
            pl.BlockSpec((C, wk), lambda c, h: (c, kq + h)),
            pl.BlockSpec((C, wv), lambda c, h: (c, kv + h)),
            pl.BlockSpec((C, wv), lambda c, h: (c, kr + h)),
            pl.BlockSpec((C, LANES), lambda c, h: (c, 0)),
            pl.BlockSpec((LANES, wk), lambda c, h: (0, h)),
            pl.BlockSpec((1, wk), lambda c, h: (0, h)),
            pl.BlockSpec((1, GLA_DV), lambda c, h: (0, 0)),
            pl.BlockSpec(fmat.shape, lambda c, h: (0, 0)),
            *cast_specs,
        ],
        out_specs=[pl.BlockSpec((C, wv), lambda c, h: (c, h)), *cast_specs],
        out_shape=[jax.ShapeDtypeStruct((S, D), BF16), *(jax.ShapeDtypeStruct(w.shape, BF16) for w in cast_srcs)],
        scratch_shapes=[pltpu.VMEM((H, GLA_DK, GLA_DV), F32)],
        compiler_params=_params("arbitrary", "arbitrary"),
        name="gla_chunk",
    )(proj, proj, proj, proj, a_lo, w_up, b_alpha, o_norm, fmat, *cast_srcs)
    return outs[0], outs[1:]


def _split3(x):
    hi = x.astype(BF16)
    r1 = x - hi.astype(F32)
    mid = r1.astype(BF16)
    lo = (r1 - mid.astype(F32)).astype(BF16)
    return hi, mid, lo


def _kvaug_body(f_ref, bf_ref, tri_ref, kn_ref, k_ref, v_ref, ka_ref, va_ref, carry_ref, *, T, H):
    @pl.when(pl.program_id(0) == 0)
    def _():
        carry_ref[...] = jnp.zeros_like(carry_ref)

    lf = _log_sigmoid(f_ref[...] + bf_ref[...])
    cs = jnp.dot(tri_ref[...], jnp.concatenate(_split3(lf), axis=1), preferred_element_type=F32)
    cum = cs[:, :LANES] + cs[:, LANES:2 * LANES] + cs[:, 2 * LANES:] + carry_ref[...]
    carry_ref[...] = cum[T - 1:T, :]
    lane = lax.broadcasted_iota(jnp.int32, (T, LANES), 1)
    v_tail = jnp.where(lane == 0, 1.0, 0.0).astype(BF16)
    for h in range(H):
        val = cum[:, h:h + 1] * (-LOG2E)
        hi = val.astype(BF16).astype(F32)
        mid = (val - hi).astype(BF16).astype(F32)
        lo = val - hi - mid
        k_tail = jnp.where(lane == 0, hi, jnp.where(lane == 1, mid, jnp.where(lane == 2, lo, 0.0)))
        head = slice(h * FOX_HEAD, (h + 1) * FOX_HEAD)
        kh = k_ref[:, head].astype(F32)
        kh = kh * lax.rsqrt(jnp.mean(kh * kh, axis=-1, keepdims=True) + EPS) * kn_ref[...]
        ka_ref[:, 2 * h * LANES:(2 * h + 1) * LANES] = kh.astype(BF16)
        ka_ref[:, (2 * h + 1) * LANES:(2 * h + 2) * LANES] = k_tail.astype(BF16)
        va_ref[:, 2 * h * LANES:(2 * h + 1) * LANES] = v_ref[:, head]
        va_ref[:, (2 * h + 1) * LANES:(2 * h + 2) * LANES] = v_tail


def _fox_kv_augment(f_logit, b_f, k_gain, kv, D):
    S = f_logit.shape[0]
    H = D // FOX_HEAD
    T = min(256, S)
    tri = jnp.asarray(np.tril(np.ones((T, T), np.float32)), BF16)
    aug = jax.ShapeDtypeStruct((S, 2 * D), BF16)
    return pl.pallas_call(
        functools.partial(_kvaug_body, T=T, H=H),
        grid=(S // T,),
        in_specs=[pl.BlockSpec((T, LANES), lambda i: (i, 0)),
                  pl.BlockSpec((1, LANES), lambda i: (0, 0)),
                  pl.BlockSpec((T, T), lambda i: (0, 0)),
                  pl.BlockSpec((1, FOX_HEAD), lambda i: (0, 0)),
                  pl.BlockSpec((T, D), lambda i: (i, 0)),
                  pl.BlockSpec((T, D), lambda i: (i, 1))],
        out_specs=[pl.BlockSpec((T, 2 * D), lambda i: (i, 0)), pl.BlockSpec((T, 2 * D), lambda i: (i, 0))],
        out_shape=[aug, aug],
        scratch_shapes=[pltpu.VMEM((1, LANES), F32)],
        compiler_params=_params("arbitrary"),
        name="fox_kv_augment",
    )(f_logit, b_f, tri, k_gain, kv, kv)


def _fox_body(q_ref, k_ref, v_ref, og_ref, o_ref, qa_ref, sa_ref, sb_ref, xa_ref, xb_ref, m_ref, acc_ref,
              *, tk, tr, G):
    i = pl.program_id(1)
    nsub = tk // tr
    bufs = ((sa_ref, xa_ref), (sb_ref, xb_ref))
    lane_q = lax.broadcasted_iota(jnp.int32, (G * tk, LANES), 1)
    qa_ref[:, :FOX_HEAD] = q_ref[...]
    qa_ref[:, FOX_HEAD:] = jnp.where(lane_q < 3, 1.0, 0.0).astype(BF16)
    m_ref[...] = jnp.full(m_ref.shape, NEG_BIG, F32)
    acc_ref[...] = jnp.zeros(acc_ref.shape, F32)

    def logits(g, cidx, buf):
        s_ref, x_ref = bufs[buf]
        kc = k_ref[pl.ds(pl.multiple_of(cidx * tk, tk), tk), :]
        for r in range(nsub):
            rows = slice(r * tr, (r + 1) * tr)
            s = lax.dot_general(qa_ref[g * tk + r * tr:g * tk + (r + 1) * tr, :], kc, (((1,), (1,)), ((), ())),
                                preferred_element_type=F32)
            s_ref[rows, :] = s
            x_ref[rows, :] = jnp.max(s, axis=-1, keepdims=True)

    def softmax_pv(g, cidx, buf, masked):
        s_ref, x_ref = bufs[buf]
        c0 = pl.multiple_of(cidx * tk, tk)
        for r in range(nsub):
            rows = slice(r * tr, (r + 1) * tr)
            grows = slice(g * tk + r * tr, g * tk + (r + 1) * tr)
            if masked:
                ncol = (r + 1) * tr
                row = lax.broadcasted_iota(jnp.int32, (tr, ncol), 0) + r * tr
                col = lax.broadcasted_iota(jnp.int32, (tr, ncol), 1)
                s = jnp.where(row >= col, s_ref[rows, :ncol], NEG_BIG)
                s_max = jnp.max(s, axis=-1, keepdims=True)
            else:
                ncol = tk
                s = s_ref[rows, :]
                s_max = x_ref[rows, :]
            vc = v_ref[pl.ds(c0, ncol), :]
            m_old = m_ref[grows, :]
            m_new = jnp.maximum(m_old, s_max)
            m_ref[grows, :] = m_new
            p = jnp.exp2(s - m_new).astype(BF16)
            acc_ref[grows, :] = (jnp.exp2(m_old - m_new) * acc_ref[grows, :]
                                 + jnp.dot(p, vc, preferred_element_type=F32))

    def full_blocks(cidx, carry):
        for g in range(G):
            if g + 1 < G:
                logits(g + 1, cidx, (g + 1) % 2)
            else:
                logits(0, cidx + 1, 0)
            softmax_pv(g, cidx, g % 2, False)
        return carry

    logits(0, 0, 0)
    lax.fori_loop(0, G * i, full_blocks, 0)
    tail = [(g, G * i + d, g == d) for d in range(G) for g in range(d, G)]
    for t, (g, cidx, masked) in enumerate(tail):
        if t + 1 < len(tail):
            logits(tail[t + 1][0], tail[t + 1][1], (t + 1) % 2)
        softmax_pv(g, cidx, t % 2, masked)

    for g in range(G):
        for r in range(nsub):
            rows = slice(g * tk + r * tr, g * tk + (r + 1) * tr)
            gate = jax.nn.sigmoid(og_ref[rows, :].astype(F32))
            inv_l = 1.0 / acc_ref[rows, FOX_HEAD:FOX_HEAD + 1]
            o_ref[rows, :] = (acc_ref[rows, :FOX_HEAD] * inv_l * gate).astype(o_ref.dtype)


def _fox_attention(qg, k_aug, v_aug, D):
    S = qg.shape[0]
    H = D // FOX_HEAD
    G = FOX_GROUPS
    tk = min(FOX_BLOCK, S // G)
    tq = G * tk
    return pl.pallas_call(
        functools.partial(_fox_body, tk=tk, tr=min(FOX_SUB, tk), G=G),
        grid=(H, S // tq),
        in_specs=[
            pl.BlockSpec((tq, FOX_HEAD), lambda h, i: (i, h)),
            pl.BlockSpec((S, 2 * LANES), lambda h, i: (0, h)),
            pl.BlockSpec((S, 2 * LANES), lambda h, i: (0, h)),
            pl.BlockSpec((tq, FOX_HEAD), lambda h, i: (i, H + h)),
        ],
        out_specs=pl.BlockSpec((tq, FOX_HEAD), lambda h, i: (i, h)),
        out_shape=jax.ShapeDtypeStruct((S, D), BF16),
        scratch_shapes=[pltpu.VMEM((tq, 2 * LANES), BF16), pltpu.VMEM((tk, tk), F32), pltpu.VMEM((tk, tk), F32),
                        pltpu.VMEM((tk, 1), F32), pltpu.VMEM((tk, 1), F32),
                        pltpu.VMEM((tq, 1), F32), pltpu.VMEM((tq, 2 * FOX_HEAD), F32)],
        compiler_params=_params("arbitrary", "arbitrary"),
        name="fox_attention",
    )(qg, k_aug, v_aug, qg)


def _pad_cols(w, n):
    return jnp.pad(w, ((0, 0), (0, n - w.shape[1])))


def _pad_rows(w, n):
    return jnp.pad(w, ((0, n - w.shape[0]), (0, 0)))


def _mlp(x, gain, shift, scale, gate, w_up, w_down, layer):
    h = _norm_mod(x, (gain, shift, scale))
    u = _matmul(h, _Weight(w_up, layer), out_dtype=BF16, epilogue=_ep_sqrelu, name=f"mlp_up_l{layer}")
    return _matmul_resid(u, _Weight(w_down, layer), x, gate, name=f"mlp_down_l{layer}")


def kernel(x, c, ada_w, ada_b, norm_mix, norm_mlp, w_mlp_up, w_mlp_down, gla_w_in, gla_w_alpha_up,
           gla_b_alpha, gla_o_norm, gla_w_out, kv_ada_w, kv_ada_b, kv_norm, kv_w, fox_b_f, fox_k_norm,
           fox_w_in, fox_q_norm, fox_w_out):
    B, S, D = x.shape
    assert B == 1 and ada_w.shape[0] == 2
    xs = x[0]
    c_col = c.reshape(D, 1)
    KW = (D // GLA_DV) * GLA_DK

    sh1, sc1, g1, sh2, sc2, g2 = jnp.split(_ada(c_col, ada_w, ada_b[0][None], 0), 6, axis=1)
    h = _norm_mod(xs, (norm_mix[0][None], sh1, sc1))
    n_main = 2 * KW + 2 * D
    w_in_t = jnp.swapaxes(gla_w_in, 1, 2)[0].astype(BF16)
    proj = _matmul(h, _Weight(w_in_t, transposed=True, n_cols=n_main), out_dtype=BF16, name="gla_in")
    a_lo = _matmul(h, _Weight(_pad_rows(w_in_t[n_main:], LANES), transposed=True), out_dtype=F32,
                   name="gla_in_lowrank")
    w_up = jnp.pad(gla_w_alpha_up[0], ((0, LANES - gla_w_alpha_up.shape[1]), (0, 0))).astype(BF16)
    y, (w_up_b, w_down_b, w_gout_b, w_fout_b) = _gla(
        proj, a_lo, w_up, gla_b_alpha[0][None], gla_o_norm[0][None], D,
        cast_srcs=(w_mlp_up.reshape(-1, w_mlp_up.shape[-1]), w_mlp_down.reshape(-1, w_mlp_down.shape[-1]),
                   gla_w_out[0], fox_w_out[0]))
    w_up_b = w_up_b.reshape(w_mlp_up.shape)
    w_down_b = w_down_b.reshape(w_mlp_down.shape)
    xs = _matmul_resid(y, _Weight(w_gout_b), xs, g1, name="gla_out")
    xs = _mlp(xs, norm_mlp[0][None], sh2, sc2, g2, w_up_b, w_down_b, 0)

    kshift, kscale = jnp.split(_ada(c_col, kv_ada_w, kv_ada_b[None], 0), 2, axis=1)
    sh1, sc1, g1, sh2, sc2, g2 = jnp.split(_ada(c_col, ada_w, ada_b[1][None], 1), 6, axis=1)
    hk, h = _norm_mod(xs, (kv_norm[None], kshift, kscale), (norm_mix[1][None], sh1, sc1))
    kv_w_t = kv_w.T.astype(BF16)
    kv = _matmul(hk, _Weight(kv_w_t, transposed=True, n_cols=2 * D), out_dtype=BF16, name="kv_proj")
    f_logit = _matmul(hk, _Weight(_pad_rows(kv_w_t[2 * D:], LANES), transposed=True), out_dtype=F32,
                      name="kv_forget")
    k_aug, v_aug = _fox_kv_augment(f_logit, _pad_cols(fox_b_f[None], LANES), fox_k_norm[None], kv, D)

    qg = _matmul_headnorm(h, _Weight(fox_w_in, 0), fox_q_norm[0][None], D, FOX_HEAD ** -0.5 * LOG2E,
                          name="fox_in")
    o = _fox_attention(qg, k_aug, v_aug, D)
    xs = _matmul_resid(o, _Weight(w_fout_b), xs, g1, name="fox_out")
    xs = _mlp(xs, norm_mlp[1][None], sh2, sc2, g2, w_up_b, w_down_b, 1)
    return xs[None]
```

```python
import functools
from typing import NamedTuple

import numpy as np
import jax
import jax.numpy as jnp
from jax import lax
from jax.experimental import pallas as pl
from jax.experimental.pallas import tpu as pltpu

EPS = 1e-6
LANES = 128
BF16_SUBLANES = 16
GLA_DK = 256
GLA_DV = 512
GLA_TAU = 16.0
GLA_CHUNK = 128
GLA_HEADS_PER_STEP = 8
FOX_HEAD = 128
FOX_BLOCK = 1024
FOX_GROUPS = 2
FOX_SUB = 512
NEG_BIG = -1e30
VMEM_LIMIT_BYTES = 56 * 1024 * 1024
MM_TILE_K = 2048
MM_FULL_K = 4096
MM_VMEM_BUDGET = 44 * 1024 * 1024
MM_MIN_TILE_N = 256
MM_ROW_SPLIT = 2
LOG2E = 1.4426950408889634

F32 = jnp.float32
BF16 = jnp.bfloat16


def _params(*sem):
    return pltpu.CompilerParams(dimension_semantics=sem, vmem_limit_bytes=VMEM_LIMIT_BYTES)


def _log_sigmoid(x):
    return jnp.minimum(x, 0.0) - jnp.log(1.0 + jnp.exp(-jnp.abs(x)))


def _silu(x):
    return x * jax.nn.sigmoid(x)


def _ada_body(c_ref, w_ref, b_ref, o_ref, cb_ref, *, tn):
    @pl.when(pl.program_id(0) == 0)
    def _():
        cc = c_ref[...]
        cb_ref[...] = jnp.broadcast_to(_silu(cc), cb_ref.shape)
    cb = cb_ref[...]
    parts = []
    for a in range(tn // LANES):
        parts.append(jnp.sum(w_ref[:, a * LANES:(a + 1) * LANES] * cb, axis=0, keepdims=True))
    o_ref[...] = jnp.concatenate(parts, axis=1) + b_ref[...]


def _ada(c_col, w, b_row, layer):
    D = c_col.shape[0]
    N = w.shape[-1]
    tn = min(512, N)
    if w.ndim == 3:
        w_spec = pl.BlockSpec((None, D, tn), lambda j: (layer, 0, j))
    else:
        w_spec = pl.BlockSpec((D, tn), lambda j: (0, j))
    return pl.pallas_call(
        functools.partial(_ada_body, tn=tn),
        grid=(N // tn,),
        in_specs=[pl.BlockSpec((D, 1), lambda j: (0, 0)), w_spec,
                  pl.BlockSpec((1, tn), lambda j: (0, j))],
        out_specs=pl.BlockSpec((1, tn), lambda j: (0, j)),
        out_shape=jax.ShapeDtypeStruct((1, N), F32),
        scratch_shapes=[pltpu.VMEM((D, LANES), F32)],
        compiler_params=_params("arbitrary"),
        name="ada_gemv",
    )(c_col, w, b_row)


def _normmod_body(x_ref, *refs):
    n = len(refs) // 4
    x = x_ref[...]
    y = x * lax.rsqrt(jnp.mean(x * x, axis=-1, keepdims=True) + EPS)
    for t in range(n):
        g_ref, sh_ref, sc_ref = refs[3 * t:3 * t + 3]
        o_ref = refs[3 * n + t]
        o_ref[...] = (y * g_ref[...] * (1.0 + sc_ref[...]) + sh_ref[...]).astype(o_ref.dtype)


def _norm_mod(x, *param_sets):
    S, D = x.shape
    tr = min(512, S)
    row = pl.BlockSpec((1, D), lambda i: (0, 0))
    tile = pl.BlockSpec((tr, D), lambda i: (i, 0))
    outs = pl.pallas_call(
        _normmod_body,
        grid=(S // tr,),
        in_specs=[tile] + [row] * (3 * len(param_sets)),
        out_specs=[tile] * len(param_sets),
        out_shape=[jax.ShapeDtypeStruct((S, D), BF16)] * len(param_sets),
        compiler_params=_params("parallel"),
        name="norm_mod",
    )(x, *(p for ps in param_sets for p in ps))
    return outs[0] if len(param_sets) == 1 else outs


def _ep_cast(acc, extra, o_ref, rows):
    o_ref[rows, :] = acc.astype(o_ref.dtype)


def _ep_sqrelu(acc, extra, o_ref, rows):
    r = jnp.maximum(acc, 0.0)
    o_ref[rows, :] = (r * r).astype(o_ref.dtype)


def _ep_resid(acc, extra, o_ref, rows):
    x_ref, g_ref = extra
    o_ref[rows, :] = x_ref[rows, :] + g_ref[...] * acc


def _ep_headnorm(acc, extra, o_ref, rows, *, scale):
    (gain_ref,) = extra
    gain = gain_ref[...]
    for g in range(acc.shape[1] // FOX_HEAD):
        blk = acc[:, g * FOX_HEAD:(g + 1) * FOX_HEAD]
        ms = jnp.mean(blk * blk, axis=-1, keepdims=True)
        y = blk * lax.rsqrt(ms + EPS) * gain * scale
        o_ref[rows, g * FOX_HEAD:(g + 1) * FOX_HEAD] = y.astype(o_ref.dtype)


def _mm_body(*refs, nk, n_extra, epilogue, b_dim, alt_epilogue, alt_from):
    a_ref, b_ref = refs[0], refs[1]
    extra = refs[2:2 + n_extra]
    o_ref = refs[2 + n_extra]
    tm = a_ref.shape[0]
    part = tm // MM_ROW_SPLIT if tm % (MM_ROW_SPLIT * BF16_SUBLANES) == 0 else tm
    parts = [slice(r0, r0 + part) for r0 in range(0, tm, part)]

    def dot(rows):
        return lax.dot_general(a_ref[rows, :], b_ref[...].astype(BF16), (((1,), (b_dim,)), ((), ())),
                               preferred_element_type=F32)

    if nk == 1:
        def run(ep):
            for rows in parts:
                ep(dot(rows), extra, o_ref, rows)

        if alt_epilogue is None:
            run(epilogue)
        else:
            j = pl.program_id(1)
            pl.when(j < alt_from)(lambda: run(epilogue))
            pl.when(j >= alt_from)(lambda: run(alt_epilogue))
        return
    acc_ref = refs[3 + n_extra]
    k = pl.program_id(2)

    @pl.when(k == 0)
    def _():
        for rows in parts:
            acc_ref[rows, :] = dot(rows)

    @pl.when((k > 0) & (k < nk - 1))
    def _():
        for rows in parts:
            acc_ref[rows, :] += dot(rows)

    @pl.when(k == nk - 1)
    def _():
        for rows in parts:
            epilogue(acc_ref[rows, :] + dot(rows), extra, o_ref, rows)


class _Weight(NamedTuple):
    array: jax.Array
    layer: int | None = None
    transposed: bool = False
    n_cols: int | None = None

    @property
    def shape(self):
        if self.transposed:
            return self.array.shape[1], self.n_cols or self.array.shape[0]
        return self.array.shape[-2:]


def _mm_tiles(M, w, out_dtype, resid=False):
    K, N = w.shape
    tk = K if K <= MM_FULL_K else MM_TILE_K
    nk = K // tk
    out_bytes = jnp.dtype(out_dtype).itemsize
    w_bytes = w.array.dtype.itemsize
    best = None
    for tm in (1024, 512, 256):
        for tn in (1024, 512, 256, LANES):
            if tm > M or M % tm or tn > N or N % tn or (tn < MM_MIN_TILE_N and tn < N):
                continue
            need = 2 * tm * tk * 2 + 2 * tk * tn * w_bytes + 2 * tm * tn * out_bytes
            need += (2 * tm * tn * 4 if resid else 0) + (tm * tn * 4 if nk > 1 else 0)
            if need > MM_VMEM_BUDGET:
                continue
            traffic = M * K * 2 * (N // tn if nk > 1 else 1) + K * N * w_bytes * (M // tm)
            if best is None or (traffic, -tn) < best[0]:
                best = ((traffic, -tn), (tm, tn, tk))
    return best[1]


def _matmul(a, w, *, out_dtype, epilogue=_ep_cast, extras=(), extra_specs=(), tiles=None, alt_epilogue=None,
            alt_from=0, name):
    M, K = a.shape
    N = w.shape[1]
    tm, tn, tk = tiles or _mm_tiles(M, w, out_dtype)
    nk = K // tk
    scratch = [pltpu.VMEM((tm, tn), F32)] if nk > 1 else []
    if w.transposed:
        w_spec = pl.BlockSpec((tn, tk), lambda i, j, k: (j, k))
    elif w.array.ndim == 3:
        w_spec = pl.BlockSpec((None, tk, tn), lambda i, j, k: (w.layer, k, j))
    else:
        w_spec = pl.BlockSpec((tk, tn), lambda i, j, k: (k, j))
    return pl.pallas_call(
        functools.partial(_mm_body, nk=nk, n_extra=len(extras), epilogue=epilogue, b_dim=1 if w.transposed else 0,
                          alt_epilogue=alt_epilogue, alt_from=alt_from),
        grid=(M // tm, N // tn, nk),
        in_specs=[pl.BlockSpec((tm, tk), lambda i, j, k: (i, k)), w_spec, *extra_specs],
        out_specs=pl.BlockSpec((tm, tn), lambda i, j, k: (i, j)),
        out_shape=jax.ShapeDtypeStruct((M, N), out_dtype),
        scratch_shapes=scratch,
        compiler_params=_params("parallel", "parallel", "arbitrary"),
        name=name,
    )(a, w.array, *extras)


def _matmul_resid(a, w, x, gate, name):
    M, N = x.shape
    tm, tn, _ = tiles = _mm_tiles(M, w, F32, resid=True)
    return _matmul(a, w, out_dtype=F32, epilogue=_ep_resid, extras=(x, gate),
                   extra_specs=(pl.BlockSpec((tm, tn), lambda i, j, k: (i, j)),
                                pl.BlockSpec((1, tn), lambda i, j, k: (0, j))), tiles=tiles, name=name)


def _matmul_headnorm(a, w, gain, n_norm_cols, scale, name):
    tiles = _mm_tiles(a.shape[0], w, BF16)
    return _matmul(a, w, out_dtype=BF16, epilogue=functools.partial(_ep_headnorm, scale=scale), extras=(gain,),
                   extra_specs=(pl.BlockSpec((1, FOX_HEAD), lambda i, j, k: (0, 0)),), tiles=tiles,
                   alt_epilogue=_ep_cast, alt_from=n_norm_cols // tiles[1], name=name)


def _gla_prefix_matrix(C):
    L = int(np.log2(C))
    t = np.arange(C)[:, None]
    j = np.arange(C)[None, :]
    mats = []
    for l in range(L):
        n = 1 << l
        mid = ((t >> (l + 1)) << (l + 1)) + n
        upper = ((t >> l) & 1) == 1
        mats.append(np.where(upper, (j >= mid) & (j <= t), (j > t) & (j < mid)))
    mats.append(j <= t)
    mats.append(j > t)
    return np.concatenate(mats, axis=0).astype(np.float32)


def _gla_body(*refs, C, L, HP, n_cast):
    q_ref, k_ref, v_ref, r_ref, alo_ref, wup_ref, balpha_ref, onorm_ref, f_ref = refs[:9]
    cast_in = refs[9:9 + n_cast]
    o_ref = refs[9 + n_cast]
    cast_out = refs[10 + n_cast:10 + 2 * n_cast]
    state_ref = refs[10 + 2 * n_cast]
    c = pl.program_id(0)
    hb = pl.program_id(1)

    for src, dst in zip(cast_in, cast_out):
        dst[...] = src[...].astype(BF16)

    row = lax.broadcasted_iota(jnp.int32, (C, C), 0)
    col = lax.broadcasted_iota(jnp.int32, (C, C), 1)
    a_lo = alo_ref[...].astype(BF16)
    fmat = f_ref[...]

    heads = range(HP)
    kcols = [slice(j * GLA_DK, (j + 1) * GLA_DK) for j in heads]
    vcols = [slice(j * GLA_DV, (j + 1) * GLA_DV) for j in heads]

    @pl.when(c == 0)
    def _():
        for j in heads:
            state_ref[hb * HP + j] = jnp.zeros(state_ref.shape[1:], F32)

    st = [state_ref[hb * HP + j] for j in heads]
    q = [q_ref[:, kcols[j]].astype(F32) * (GLA_DK ** -0.5) for j in heads]
    k = [k_ref[:, kcols[j]].astype(F32) for j in heads]
    qb = [q_ref[:, kcols[j]] * jnp.asarray(GLA_DK ** -0.5, BF16) for j in heads]
    kb = [k_ref[:, kcols[j]] for j in heads]
    v = [v_ref[:, vcols[j]] for j in heads]

    xs = []
    for j in heads:
        z = jnp.dot(a_lo, wup_ref[:, kcols[j]], preferred_element_type=F32) + balpha_ref[:, kcols[j]]
        la = _log_sigmoid(z) * (1.0 / GLA_TAU)
        hi = la.astype(BF16)
        mid = (la - hi.astype(F32)).astype(BF16)
        xs.append(jnp.dot(fmat, jnp.concatenate([hi, mid], axis=0), preferred_element_type=F32))

    scores = [jnp.where(row == col, jnp.sum(q[j] * k[j], axis=-1, keepdims=True), 0.0) for j in heads]
    for l in range(L):
        rl = row >> l
        cl = col >> l
        pick = ((rl ^ cl) == 1) & (rl > cl)
        for j in heads:
            e = jnp.exp(xs[j][l * C:(l + 1) * C]).astype(BF16)
            s_l = lax.dot_general(qb[j] * e, kb[j] * e, (((1,), (1,)), ((), ())), preferred_element_type=F32)
            scores[j] = jnp.where(pick, s_l, scores[j])

    outs = []
    for j in heads:
        b = xs[j][L * C:(L + 1) * C]
        rem = xs[j][(L + 1) * C:]
        o = jnp.dot(scores[j].astype(BF16), v[j], preferred_element_type=F32)
        o += jnp.dot((q[j] * jnp.exp(b)).astype(BF16), st[j].astype(BF16), preferred_element_type=F32)
        outs.append(o)
        kt = jnp.transpose(k[j] * jnp.exp(rem)).astype(BF16)
        d_row = jnp.exp(b[C - 1:C, :])
        d_col = jnp.transpose(jnp.broadcast_to(d_row, (LANES, GLA_DK)))[:, :1]
        st[j] = d_col * st[j] + jnp.dot(kt, v[j], preferred_element_type=F32)

    for j in heads:
        state_ref[hb * HP + j] = st[j]
        o = outs[j]
        ms = jnp.mean(o * o, axis=-1, keepdims=True)
        y = o * lax.rsqrt(ms + EPS) * onorm_ref[...]
        o_ref[:, vcols[j]] = (y * _silu(r_ref[:, vcols[j]].astype(F32))).astype(o_ref.dtype)


def _gla(proj, a_lo, w_up, b_alpha, o_norm, D, cast_srcs=()):
    S = proj.shape[0]
    H = D // GLA_DV
    KW = H * GLA_DK
    C = min(GLA_CHUNK, S)
    NC = S // C
    L = int(np.log2(C))
    HP = min(GLA_HEADS_PER_STEP, H)
    fmat = jnp.asarray(np.tile(_gla_prefix_matrix(C), (1, 2)), BF16)
    wk, wv = HP * GLA_DK, HP * GLA_DV
    kq, kv, kr = KW // wk, (2 * KW) // wv, (2 * KW + D) // wv
    cast_specs = []
    for w in cast_srcs:
        rows = w.shape[0] // NC
        assert rows * NC == w.shape[0] and rows % BF16_SUBLANES == 0, w.shape
        cast_specs.append(pl.BlockSpec((rows, w.shape[1]), lambda c, h: (c, 0)))
    outs = pl.pallas_call(
        functools.partial(_gla_body, C=C, L=L, HP=HP, n_cast=len(cast_srcs)),
        grid=(NC, H // HP),
        in_specs=[
            pl.BlockSpec((C, wk), lambda c, h: (c, h)),
            pl.BlockSpec((C, wk), lambda c, h: (c, kq + h)),
            pl.BlockSpec((C, wv), lambda c, h: (c, kv + h)),
            pl.BlockSpec((C, wv), lambda c, h: (c, kr + h)),
            pl.BlockSpec((C, LANES), lambda c, h: (c, 0)),
            pl.BlockSpec((LANES, wk), lambda c, h: (0, h)),
            pl.BlockSpec((1, wk), lambda c, h: (0, h)),
            pl.BlockSpec((1, GLA_DV), lambda c, h: (0, 0)),
            pl.BlockSpec(fmat.shape, lambda c, h: (0, 0)),
            *cast_specs,
        ],
        out_specs=[pl.BlockSpec((C, wv), lambda c, h: (c, h)), *cast_specs],
        out_shape=[jax.ShapeDtypeStruct((S, D), BF16), *(jax.ShapeDtypeStruct(w.shape, BF16) for w in cast_srcs)],
        scratch_shapes=[pltpu.VMEM((H, GLA_DK, GLA_DV), F32)],
        compiler_params=_params("arbitrary", "arbitrary"),
        name="gla_chunk",
    )(proj, proj, proj, proj, a_lo, w_up, b_alpha, o_norm, fmat, *cast_srcs)
    return outs[0], outs[1:]


def _split3(x):
    hi = x.astype(BF16)
    r1 = x - hi.astype(F32)
    mid = r1.astype(BF16)
    lo = (r1 - mid.astype(F32)).astype(BF16)
    return hi, mid, lo


def _kvaug_body(f_ref, bf_ref, tri_ref, kn_ref, k_ref, v_ref, ka_ref, va_ref, carry_ref, *, T, H):
    @pl.when(pl.program_id(0) == 0)
    def _():
        carry_ref[...] = jnp.zeros_like(carry_ref)

    lf = _log_sigmoid(f_ref[...] + bf_ref[...])
    cs = jnp.dot(tri_ref[...], jnp.concatenate(_split3(lf), axis=1), preferred_element_type=F32)
    cum = cs[:, :LANES] + cs[:, LANES:2 * LANES] + cs[:, 2 * LANES:] + carry_ref[...]
    carry_ref[...] = cum[T - 1:T, :]
    lane = lax.broadcasted_iota(jnp.int32, (T, LANES), 1)
    v_tail = jnp.where(lane == 0, 1.0, 0.0).astype(BF16)
    for h in range(H):
        val = cum[:, h:h + 1] * (-LOG2E)
        hi = val.astype(BF16).astype(F32)
        mid = (val - hi).astype(BF16).astype(F32)
        lo = val - hi - mid
        k_tail = jnp.where(lane == 0, hi, jnp.where(lane == 1, mid, jnp.where(lane == 2, lo, 0.0)))
        head = slice(h * FOX_HEAD, (h + 1) * FOX_HEAD)
        kh = k_ref[:, head].astype(F32)
        kh = kh * lax.rsqrt(jnp.mean(kh * kh, axis=-1, keepdims=True) + EPS) * kn_ref[...]
        ka_ref[:, 2 * h * LANES:(2 * h + 1) * LANES] = kh.astype(BF16)
        ka_ref[:, (2 * h + 1) * LANES:(2 * h + 2) * LANES] = k_tail.astype(BF16)
        va_ref[:, 2 * h * LANES:(2 * h + 1) * LANES] = v_ref[:, head]
        va_ref[:, (2 * h + 1) * LANES:(2 * h + 2) * LANES] = v_tail


def _fox_kv_augment(f_logit, b_f, k_gain, kv, D):
    S = f_logit.shape[0]
    H = D // FOX_HEAD
    T = min(256, S)
    tri = jnp.asarray(np.tril(np.ones((T, T), np.float32)), BF16)
    aug = jax.ShapeDtypeStruct((S, 2 * D), BF16)
    return pl.pallas_call(
        functools.partial(_kvaug_body, T=T, H=H),
        grid=(S // T,),
        in_specs=[pl.BlockSpec((T, LANES), lambda i: (i, 0)),
                  pl.BlockSpec((1, LANES), lambda i: (0, 0)),
                  pl.BlockSpec((T, T), lambda i: (0, 0)),
                  pl.BlockSpec((1, FOX_HEAD), lambda i: (0, 0)),
                  pl.BlockSpec((T, D), lambda i: (i, 0)),
                  pl.BlockSpec((T, D), lambda i: (i, 1))],
        out_specs=[pl.BlockSpec((T, 2 * D), lambda i: (i, 0)), pl.BlockSpec((T, 2 * D), lambda i: (i, 0))],
        out_shape=[aug, aug],
        scratch_shapes=[pltpu.VMEM((1, LANES), F32)],
        compiler_params=_params("arbitrary"),
        name="fox_kv_augment",
    )(f_logit, b_f, tri, k_gain, kv, kv)


def _fox_body(q_ref, k_ref, v_ref, og_ref, o_ref, qa_ref, sa_ref, sb_ref, xa_ref, xb_ref, m_ref, acc_ref,
              *, tk, tr, G):
    i = pl.program_id(1)
    nsub = tk // tr
    bufs = ((sa_ref, xa_ref), (sb_ref, xb_ref))
    lane_q = lax.broadcasted_iota(jnp.int32, (G * tk, LANES), 1)
    qa_ref[:, :FOX_HEAD] = q_ref[...]
    qa_ref[:, FOX_HEAD:] = jnp.where(lane_q < 3, 1.0, 0.0).astype(BF16)
    m_ref[...] = jnp.full(m_ref.shape, NEG_BIG, F32)
    acc_ref[...] = jnp.zeros(acc_ref.shape, F32)

    def logits(g, cidx, buf):
        s_ref, x_ref = bufs[buf]
        kc = k_ref[pl.ds(pl.multiple_of(cidx * tk, tk), tk), :]
        for r in range(nsub):
            rows = slice(r * tr, (r + 1) * tr)
            s = lax.dot_general(qa_ref[g * tk + r * tr:g * tk + (r + 1) * tr, :], kc, (((1,), (1,)), ((), ())),
                                preferred_element_type=F32)
            s_ref[rows, :] = s
            x_ref[rows, :] = jnp.max(s, axis=-1, keepdims=True)

    def softmax_pv(g, cidx, buf, masked):
        s_ref, x_ref = bufs[buf]
        c0 = pl.multiple_of(cidx * tk, tk)
        for r in range(nsub):
            rows = slice(r * tr, (r + 1) * tr)
            grows = slice(g * tk + r * tr, g * tk + (r + 1) * tr)
            if masked:
                ncol = (r + 1) * tr
                row = lax.broadcasted_iota(jnp.int32, (tr, ncol), 0) + r * tr
                col = lax.broadcasted_iota(jnp.int32, (tr, ncol), 1)
                s = jnp.where(row >= col, s_ref[rows, :ncol], NEG_BIG)
                s_max = jnp.max(s, axis=-1, keepdims=True)
            else:
                ncol = tk
                s = s_ref[rows, :]
                s_max = x_ref[rows, :]
            vc = v_ref[pl.ds(c0, ncol), :]
            m_old = m_ref[grows, :]
            m_new = jnp.maximum(m_old, s_max)
            m_ref[grows, :] = m_new
            p = jnp.exp2(s - m_new).astype(BF16)
            acc_ref[grows, :] = (jnp.exp2(m_old - m_new) * acc_ref[grows, :]
                                 + jnp.dot(p, vc, preferred_element_type=F32))

    def full_blocks(cidx, carry):
        for g in range(G):
            if g + 1 < G:
                logits(g + 1, cidx, (g + 1) % 2)
            else:
                logits(0, cidx + 1, 0)
            softmax_pv(g, cidx, g % 2, False)
        return carry

    logits(0, 0, 0)
    lax.fori_loop(0, G * i, full_blocks, 0)
    tail = [(g, G * i + d, g == d) for d in range(G) for g in range(d, G)]
    for t, (g, cidx, masked) in enumerate(tail):
        if t + 1 < len(tail):
            logits(tail[t + 1][0], tail[t + 1][1], (t + 1) % 2)
        softmax_pv(g, cidx, t % 2, masked)

    for g in range(G):
        for r in range(nsub):
            rows = slice(g * tk + r * tr, g * tk + (r + 1) * tr)
            gate = jax.nn.sigmoid(og_ref[rows, :].astype(F32))
            inv_l = 1.0 / acc_ref[rows, FOX_HEAD:FOX_HEAD + 1]
            o_ref[rows, :] = (acc_ref[rows, :FOX_HEAD] * inv_l * gate).astype(o_ref.dtype)


def _fox_attention(qg, k_aug, v_aug, D):
    S = qg.shape[0]
    H = D // FOX_HEAD
    G = FOX_GROUPS
    tk = min(FOX_BLOCK, S // G)
    tq = G * tk
    return pl.pallas_call(
        functools.partial(_fox_body, tk=tk, tr=min(FOX_SUB, tk), G=G),
        grid=(H, S // tq),
        in_specs=[
            pl.BlockSpec((tq, FOX_HEAD), lambda h, i: (i, h)),
            pl.BlockSpec((S, 2 * LANES), lambda h, i: (0, h)),
            pl.BlockSpec((S, 2 * LANES), lambda h, i: (0, h)),
            pl.BlockSpec((tq, FOX_HEAD), lambda h, i: (i, H + h)),
        ],
        out_specs=pl.BlockSpec((tq, FOX_HEAD), lambda h, i: (i, h)),
        out_shape=jax.ShapeDtypeStruct((S, D), BF16),
        scratch_shapes=[pltpu.VMEM((tq, 2 * LANES), BF16), pltpu.VMEM((tk, tk), F32), pltpu.VMEM((tk, tk), F32),
                        pltpu.VMEM((tk, 1), F32), pltpu.VMEM((tk, 1), F32),
                        pltpu.VMEM((tq, 1), F32), pltpu.VMEM((tq, 2 * FOX_HEAD), F32)],
        compiler_params=_params("arbitrary", "arbitrary"),
        name="fox_attention",
    )(qg, k_aug, v_aug, qg)


def _pad_cols(w, n):
    return jnp.pad(w, ((0, 0), (0, n - w.shape[1])))


def _pad_rows(w, n):
    return jnp.pad(w, ((0, n - w.shape[0]), (0, 0)))


def _mlp(x, gain, shift, scale, gate, w_up, w_down, layer):
    h = _norm_mod(x, (gain, shift, scale))
    u = _matmul(h, _Weight(w_up, layer), out_dtype=BF16, epilogue=_ep_sqrelu, name=f"mlp_up_l{layer}")
    return _matmul_resid(u, _Weight(w_down, layer), x, gate, name=f"mlp_down_l{layer}")


def kernel(x, c, ada_w, ada_b, norm_mix, norm_mlp, w_mlp_up, w_mlp_down, gla_w_in, gla_w_alpha_up,
           gla_b_alpha, gla_o_norm, gla_w_out, kv_ada_w, kv_ada_b, kv_norm, kv_w, fox_b_f, fox_k_norm,
           fox_w_in, fox_q_norm, fox_w_out):
    B, S, D = x.shape
    assert B == 1 and ada_w.shape[0] == 2
    xs = x[0]
    c_col = c.reshape(D, 1)
    KW = (D // GLA_DV) * GLA_DK

    sh1, sc1, g1, sh2, sc2, g2 = jnp.split(_ada(c_col, ada_w, ada_b[0][None], 0), 6, axis=1)
    h = _norm_mod(xs, (norm_mix[0][None], sh1, sc1))
    n_main = 2 * KW + 2 * D
    w_in_t = jnp.swapaxes(gla_w_in, 1, 2)[0].astype(BF16)
    proj = _matmul(h, _Weight(w_in_t, transposed=True, n_cols=n_main), out_dtype=BF16, name="gla_in")
    a_lo = _matmul(h, _Weight(_pad_rows(w_in_t[n_main:], LANES), transposed=True), out_dtype=F32,
                   name="gla_in_lowrank")
    w_up = jnp.pad(gla_w_alpha_up[0], ((0, LANES - gla_w_alpha_up.shape[1]), (0, 0))).astype(BF16)
    y, (w_up_b, w_down_b, w_gout_b, w_fout_b) = _gla(
        proj, a_lo, w_up, gla_b_alpha[0][None], gla_o_norm[0][None], D,
        cast_srcs=(w_mlp_up.reshape(-1, w_mlp_up.shape[-1]), w_mlp_down.reshape(-1, w_mlp_down.shape[-1]),
                   gla_w_out[0], fox_w_out[0]))
    w_up_b = w_up_b.reshape(w_mlp_up.shape)
    w_down_b = w_down_b.reshape(w_mlp_down.shape)
    xs = _matmul_resid(y, _Weight(w_gout_b), xs, g1, name="gla_out")
    xs = _mlp(xs, norm_mlp[0][None], sh2, sc2, g2, w_up_b, w_down_b, 0)

    kshift, kscale = jnp.split(_ada(c_col, kv_ada_w, kv_ada_b[None], 0), 2, axis=1)
    sh1, sc1, g1, sh2, sc2, g2 = jnp.split(_ada(c_col, ada_w, ada_b[1][None], 1), 6, axis=1)
    hk, h = _norm_mod(xs, (kv_norm[None], kshift, kscale), (norm_mix[1][None], sh1, sc1))
    kv_w_t = kv_w.T.astype(BF16)
    kv = _matmul(hk, _Weight(kv_w_t, transposed=True, n_cols=2 * D), out_dtype=BF16, name="kv_proj")
    f_logit = _matmul(hk, _Weight(_pad_rows(kv_w_t[2 * D:], LANES), transposed=True), out_dtype=F32,
                      name="kv_forget")
    k_aug, v_aug = _fox_kv_augment(f_logit, _pad_cols(fox_b_f[None], LANES), fox_k_norm[None], kv, D)

    qg = _matmul_headnorm(h, _Weight(fox_w_in, 0), fox_q_norm[0][None], D, FOX_HEAD ** -0.5 * LOG2E,
                          name="fox_in")
    o = _fox_attention(qg, k_aug, v_aug, D)
    xs = _matmul_resid(o, _Weight(w_fout_b), xs, g1, name="fox_out")
    xs = _mlp(xs, norm_mlp[1][None], sh2, sc2, g2, w_up_b, w_down_b, 1)
    return xs[None]
```

```python
import functools
from typing import NamedTuple

import numpy as np
import jax
import jax.numpy as jnp
from jax import lax
from jax.experimental import pallas as pl
from jax.experimental.pallas import tpu as pltpu

EPS = 1e-6
LANES = 128
BF16_SUBLANES = 16
GLA_DK = 256
GLA_DV = 512
GLA_TAU = 16.0
GLA_CHUNK = 128
GLA_HEADS_PER_STEP = 8
FOX_HEAD = 128
FOX_BLOCK = 1024
FOX_GROUPS = 2
FOX_SUB = 512
NEG_BIG = -1e30
VMEM_LIMIT_BYTES = 56 * 1024 * 1024
MM_TILE_K = 2048
MM_FULL_K = 4096
MM_VMEM_BUDGET = 44 * 1024 * 1024
MM_MIN_TILE_N = 256
MM_ROW_SPLIT = 2
LOG2E = 1.4426950408889634

F32 = jnp.float32
BF16 = jnp.bfloat16


def _params(*sem):
    return pltpu.CompilerParams(dimension_semantics=sem, vmem_limit_bytes=VMEM_LIMIT_BYTES)


def _log_sigmoid(x):
    return jnp.minimum(x, 0.0) - jnp.log(1.0 + jnp.exp(-jnp.abs(x)))


def _silu(x):
    return x * jax.nn.sigmoid(x)


def _ada_body(c_ref, w_ref, b_ref, o_ref, cb_ref, *, tn):
    @pl.when(pl.program_id(0) == 0)
    def _():
        cc = c_ref[...]
        cb_ref[...] = jnp.broadcast_to(_silu(cc), cb_ref.shape)
    cb = cb_ref[...]
    parts = []
    for a in range(tn // LANES):
        parts.append(jnp.sum(w_ref[:, a * LANES:(a + 1) * LANES] * cb, axis=0, keepdims=True))
    o_ref[...] = jnp.concatenate(parts, axis=1) + b_ref[...]


def _ada(c_col, w, b_row, layer):
    D = c_col.shape[0]
    N = w.shape[-1]
    tn = min(512, N)
    if w.ndim == 3:
        w_spec = pl.BlockSpec((None, D, tn), lambda j: (layer, 0, j))
    else:
        w_spec = pl.BlockSpec((D, tn), lambda j: (0, j))
    return pl.pallas_call(
        functools.partial(_ada_body, tn=tn),
        grid=(N // tn,),
        in_specs=[pl.BlockSpec((D, 1), lambda j: (0, 0)), w_spec,
                  pl.BlockSpec((1, tn), lambda j: (0, j))],
        out_specs=pl.BlockSpec((1, tn), lambda j: (0, j)),
        out_shape=jax.ShapeDtypeStruct((1, N), F32),
        scratch_shapes=[pltpu.VMEM((D, LANES), F32)],
        compiler_params=_params("arbitrary"),
        name="ada_gemv",
    )(c_col, w, b_row)


def _normmod_body(x_ref, *refs):
    n = len(refs) // 4
    x = x_ref[...]
    y = x * lax.rsqrt(jnp.mean(x * x, axis=-1, keepdims=True) + EPS)
    for t in range(n):
        g_ref, sh_ref, sc_ref = refs[3 * t:3 * t + 3]
        o_ref = refs[3 * n + t]
        o_ref[...] = (y * g_ref[...] * (1.0 + sc_ref[...]) + sh_ref[...]).astype(o_ref.dtype)


def _norm_mod(x, *param_sets):
    S, D = x.shape
    tr = min(512, S)
    row = pl.BlockSpec((1, D), lambda i: (0, 0))
    tile = pl.BlockSpec((tr, D), lambda i: (i, 0))
    outs = pl.pallas_call(
        _normmod_body,
        grid=(S // tr,),
        in_specs=[tile] + [row] * (3 * len(param_sets)),
        out_specs=[tile] * len(param_sets),
        out_shape=[jax.ShapeDtypeStruct((S, D), BF16)] * len(param_sets),
        compiler_params=_params("parallel"),
        name="norm_mod",
    )(x, *(p for ps in param_sets for p in ps))
    return outs[0] if len(param_sets) == 1 else outs


def _ep_cast(acc, extra, o_ref, rows):
    o_ref[rows, :] = acc.astype(o_ref.dtype)


def _ep_sqrelu(acc, extra, o_ref, rows):
    r = jnp.maximum(acc, 0.0)
    o_ref[rows, :] = (r * r).astype(o_ref.dtype)


def _ep_resid(acc, extra, o_ref, rows):
    x_ref, g_ref = extra
    o_ref[rows, :] = x_ref[rows, :] + g_ref[...] * acc


def _ep_headnorm(acc, extra, o_ref, rows, *, scale):
    (gain_ref,) = extra
    gain = gain_ref[...]
    for g in range(acc.shape[1] // FOX_HEAD):
        blk = acc[:, g * FOX_HEAD:(g + 1) * FOX_HEAD]
        ms = jnp.mean(blk * blk, axis=-1, keepdims=True)
        y = blk * lax.rsqrt(ms + EPS) * gain * scale
        o_ref[rows, g * FOX_HEAD:(g + 1) * FOX_HEAD] = y.astype(o_ref.dtype)


def _mm_body(*refs, nk, n_extra, epilogue, b_dim, alt_epilogue, alt_from):
    a_ref, b_ref = refs[0], refs[1]
    extra = refs[2:2 + n_extra]
    o_ref = refs[2 + n_extra]
    tm = a_ref.shape[0]
    part = tm // MM_ROW_SPLIT if tm % (MM_ROW_SPLIT * BF16_SUBLANES) == 0 else tm
    parts = [slice(r0, r0 + part) for r0 in range(0, tm, part)]

    def dot(rows):
        return lax.dot_general(a_ref[rows, :], b_ref[...].astype(BF16), (((1,), (b_dim,)), ((), ())),
                               preferred_element_type=F32)

    if nk == 1:
        def run(ep):
            for rows in parts:
                ep(dot(rows), extra, o_ref, rows)

        if alt_epilogue is None:
            run(epilogue)
        else:
            j = pl.program_id(1)
            pl.when(j < alt_from)(lambda: run(epilogue))
            pl.when(j >= alt_from)(lambda: run(alt_epilogue))
        return
    acc_ref = refs[3 + n_extra]
    k = pl.program_id(2)

    @pl.when(k == 0)
    def _():
        for rows in parts:
            acc_ref[rows, :] = dot(rows)

    @pl.when((k > 0) & (k < nk - 1))
    def _():
        for rows in parts:
            acc_ref[rows, :] += dot(rows)

    @pl.when(k == nk - 1)
    def _():
        for rows in parts:
            epilogue(acc_ref[rows, :] + dot(rows), extra, o_ref, rows)


class _Weight(NamedTuple):
    array: jax.Array
    layer: int | None = None
    transposed: bool = False
    n_cols: int | None = None

    @property
    def shape(self):
        if self.transposed:
            return self.array.shape[1], self.n_cols or self.array.shape[0]
        return self.array.shape[-2:]


def _mm_tiles(M, w, out_dtype, resid=False):
    K, N = w.shape
    tk = K if K <= MM_FULL_K else MM_TILE_K
    nk = K // tk
    out_bytes = jnp.dtype(out_dtype).itemsize
    w_bytes = w.array.dtype.itemsize
    best = None
    for tm in (1024, 512, 256):
        for tn in (1024, 512, 256, LANES):
            if tm > M or M % tm or tn > N or N % tn or (tn < MM_MIN_TILE_N and tn < N):
                continue
            need = 2 * tm * tk * 2 + 2 * tk * tn * w_bytes + 2 * tm * tn * out_bytes
            need += (2 * tm * tn * 4 if resid else 0) + (tm * tn * 4 if nk > 1 else 0)
            if need > MM_VMEM_BUDGET:
                continue
            traffic = M * K * 2 * (N // tn if nk > 1 else 1) + K * N * w_bytes * (M // tm)
            if best is None or (traffic, -tn) < best[0]:
                best = ((traffic, -tn), (tm, tn, tk))
    return best[1]


def _matmul(a, w, *, out_dtype, epilogue=_ep_cast, extras=(), extra_specs=(), tiles=None, alt_epilogue=None,
            alt_from=0, name):
    M, K = a.shape
    N = w.shape[1]
    tm, tn, tk = tiles or _mm_tiles(M, w, out_dtype)
    nk = K // tk
    scratch = [pltpu.VMEM((tm, tn), F32)] if nk > 1 else []
    if w.transposed:
        w_spec = pl.BlockSpec((tn, tk), lambda i, j, k: (j, k))
    elif w.array.ndim == 3:
        w_spec = pl.BlockSpec((None, tk, tn), lambda i, j, k: (w.layer, k, j))
    else:
        w_spec = pl.BlockSpec((tk, tn), lambda i, j, k: (k, j))
    return pl.pallas_call(
        functools.partial(_mm_body, nk=nk, n_extra=len(extras), epilogue=epilogue, b_dim=1 if w.transposed else 0,
                          alt_epilogue=alt_epilogue, alt_from=alt_from),
        grid=(M // tm, N // tn, nk),
        in_specs=[pl.BlockSpec((tm, tk), lambda i, j, k: (i, k)), w_spec, *extra_specs],
        out_specs=pl.BlockSpec((tm, tn), lambda i, j, k: (i, j)),
        out_shape=jax.ShapeDtypeStruct((M, N), out_dtype),
        scratch_shapes=scratch,
        compiler_params=_params("parallel", "parallel", "arbitrary"),
        name=name,
    )(a, w.array, *extras)


def _matmul_resid(a, w, x, gate, name):
    M, N = x.shape
    tm, tn, _ = tiles = _mm_tiles(M, w, F32, resid=True)
    return _matmul(a, w, out_dtype=F32, epilogue=_ep_resid, extras=(x, gate),
                   extra_specs=(pl.BlockSpec((tm, tn), lambda i, j, k: (i, j)),
                                pl.BlockSpec((1, tn), lambda i, j, k: (0, j))), tiles=tiles, name=name)


def _matmul_headnorm(a, w, gain, n_norm_cols, scale, name):
    tiles = _mm_tiles(a.shape[0], w, BF16)
    return _matmul(a, w, out_dtype=BF16, epilogue=functools.partial(_ep_headnorm, scale=scale), extras=(gain,),
                   extra_specs=(pl.BlockSpec((1, FOX_HEAD), lambda i, j, k: (0, 0)),), tiles=tiles,
                   alt_epilogue=_ep_cast, alt_from=n_norm_cols // tiles[1], name=name)


def _gla_prefix_matrix(C):
    L = int(np.log2(C))
    t = np.arange(C)[:, None]
    j = np.arange(C)[None, :]
    mats = []
    for l in range(L):
        n = 1 << l
        mid = ((t >> (l + 1)) << (l + 1)) + n
        upper = ((t >> l) & 1) == 1
        mats.append(np.where(upper, (j >= mid) & (j <= t), (j > t) & (j < mid)))
    mats.append(j <= t)
    mats.append(j > t)
    return np.concatenate(mats, axis=0).astype(np.float32)


def _gla_body(*refs, C, L, HP, n_cast):
    q_ref, k_ref, v_ref, r_ref, alo_ref, wup_ref, balpha_ref, onorm_ref, f_ref = refs[:9]
    cast_in = refs[9:9 + n_cast]
    o_ref = refs[9 + n_cast]
    cast_out = refs[10 + n_cast:10 + 2 * n_cast]
    state_ref = refs[10 + 2 * n_cast]
    c = pl.program_id(0)
    hb = pl.program_id(1)

    for src, dst in zip(cast_in, cast_out):
        dst[...] = src[...].astype(BF16)

    row = lax.broadcasted_iota(jnp.int32, (C, C), 0)
    col = lax.broadcasted_iota(jnp.int32, (C, C), 1)
    a_lo = alo_ref[...].astype(BF16)
    fmat = f_ref[...]

    heads = range(HP)
    kcols = [slice(j * GLA_DK, (j + 1) * GLA_DK) for j in heads]
    vcols = [slice(j * GLA_DV, (j + 1) * GLA_DV) for j in heads]

    @pl.when(c == 0)
    def _():
        for j in heads:
            state_ref[hb * HP + j] = jnp.zeros(state_ref.shape[1:], F32)

    st = [state_ref[hb * HP + j] for j in heads]
    q = [q_ref[:, kcols[j]].astype(F32) * (GLA_DK ** -0.5) for j in heads]
    k = [k_ref[:, kcols[j]].astype(F32) for j in heads]
    qb = [q_ref[:, kcols[j]] * jnp.asarray(GLA_DK ** -0.5, BF16) for j in heads]
    kb = [k_ref[:, kcols[j]] for j in heads]
    v = [v_ref[:, vcols[j]] for j in heads]

    xs = []
    for j in heads:
        z = jnp.dot(a_lo, wup_ref[:, kcols[j]], preferred_element_type=F32) + balpha_ref[:, kcols[j]]
        la = _log_sigmoid(z) * (1.0 / GLA_TAU)
        hi = la.astype(BF16)
        mid = (la - hi.astype(F32)).astype(BF16)
        xs.append(jnp.dot(fmat, jnp.concatenate([hi, mid], axis=0), preferred_element_type=F32))

    scores = [jnp.where(row == col, jnp.sum(q[j] * k[j], axis=-1, keepdims=True), 0.0) for j in heads]
    for l in range(L):
        rl = row >> l
        cl = col >> l
        pick = ((rl ^ cl) == 1) & (rl > cl)
        for j in heads:
            e = jnp.exp(xs[j][l * C:(l + 1) * C]).astype(BF16)
            s_l = lax.dot_general(qb[j] * e, kb[j] * e, (((1,), (1,)), ((), ())), preferred_element_type=F32)
            scores[j] = jnp.where(pick, s_l, scores[j])

    outs = []
    for j in heads:
        b = xs[j][L * C:(L + 1) * C]
        rem = xs[j][(L + 1) * C:]
        o = jnp.dot(scores[j].astype(BF16), v[j], preferred_element_type=F32)
        o += jnp.dot((q[j] * jnp.exp(b)).astype(BF16), st[j].astype(BF16), preferred_element_type=F32)
        outs.append(o)
        kt = jnp.transpose(k[j] * jnp.exp(rem)).astype(BF16)
        d_row = jnp.exp(b[C - 1:C, :])
        d_col = jnp.transpose(jnp.broadcast_to(d_row, (LANES, GLA_DK)))[:, :1]
        st[j] = d_col * st[j] + jnp.dot(kt, v[j], preferred_element_type=F32)

    for j in heads:
        state_ref[hb * HP + j] = st[j]
        o = outs[j]
        ms = jnp.mean(o * o, axis=-1, keepdims=True)
        y = o * lax.rsqrt(ms + EPS) * onorm_ref[...]
        o_ref[:, vcols[j]] = (y * _silu(r_ref[:, vcols[j]].astype(F32))).astype(o_ref.dtype)


def _gla(proj, a_lo, w_up, b_alpha, o_norm, D, cast_srcs=()):
    S = proj.shape[0]
    H = D // GLA_DV
    KW = H * GLA_DK
    C = min(GLA_CHUNK, S)
    NC = S // C
    L = int(np.log2(C))
    HP = min(GLA_HEADS_PER_STEP, H)
    fmat = jnp.asarray(np.tile(_gla_prefix_matrix(C), (1, 2)), BF16)
    wk, wv = HP * GLA_DK, HP * GLA_DV
    kq, kv, kr = KW // wk, (2 * KW) // wv, (2 * KW + D) // wv
    cast_specs = []
    for w in cast_srcs:
        rows = w.shape[0] // NC
        assert rows * NC == w.shape[0] and rows % BF16_SUBLANES == 0, w.shape
        cast_specs.append(pl.BlockSpec((rows, w.shape[1]), lambda c, h: (c, 0)))
    outs = pl.pallas_call(
        functools.partial(_gla_body, C=C, L=L, HP=HP, n_cast=len(cast_srcs)),
        grid=(NC, H // HP),
        in_specs=[
            pl.BlockSpec((C, wk), lambda c, h: (c, h)),
            pl.BlockSpec((C, wk), lambda c, h: (c, kq + h)),
            pl.BlockSpec((C, wv), lambda c, h: (c, kv + h)),
            pl.BlockSpec((C, wv), lambda c, h: (c, kr + h)),
            pl.BlockSpec((C, LANES), lambda c, h: (c, 0)),
            pl.BlockSpec((LANES, wk), lambda c, h: (0, h)),
            pl.BlockSpec((1, wk), lambda c, h: (0, h)),
            pl.BlockSpec((1, GLA_DV), lambda c, h: (0, 0)),
            pl.BlockSpec(fmat.shape, lambda c, h: (0, 0)),
            *cast_specs,
        ],
        out_specs=[pl.BlockSpec((C, wv), lambda c, h: (c, h)), *cast_specs],
        out_shape=[jax.ShapeDtypeStruct((S, D), BF16), *(jax.ShapeDtypeStruct(w.shape, BF16) for w in cast_srcs)],
        scratch_shapes=[pltpu.VMEM((H, GLA_DK, GLA_DV), F32)],
        compiler_params=_params("arbitrary", "arbitrary"),
        name="gla_chunk",
    )(proj, proj, proj, proj, a_lo, w_up, b_alpha, o_norm, fmat, *cast_srcs)
    return outs[0], outs[1:]


def _split3(x):
    hi = x.astype(BF16)
    r1 = x - hi.astype(F32)
    mid = r1.astype(BF16)
    lo = (r1 - mid.astype(F32)).astype(BF16)
    return hi, mid, lo


def _kvaug_body(f_ref, bf_ref, tri_ref, kn_ref, k_ref, v_ref, ka_ref, va_ref, carry_ref, *, T, H):
    @pl.when(pl.program_id(0) == 0)
    def _():
        carry_ref[...] = jnp.zeros_like(carry_ref)

    lf = _log_sigmoid(f_ref[...] + bf_ref[...])
    cs = jnp.dot(tri_ref[...], jnp.concatenate(_split3(lf), axis=1), preferred_element_type=F32)
    cum = cs[:, :LANES] + cs[:, LANES:2 * LANES] + cs[:, 2 * LANES:] + carry_ref[...]
    carry_ref[...] = cum[T - 1:T, :]
    lane = lax.broadcasted_iota(jnp.int32, (T, LANES), 1)
    v_tail = jnp.where(lane == 0, 1.0, 0.0).astype(BF16)
    ones = jnp.ones((FOX_HEAD, LANES), BF16)
    for h in range(H):
        val = cum[:, h:h + 1] * (-LOG2E)
        hi = val.astype(BF16).astype(F32)
        mid = (val - hi).astype(BF16).astype(F32)
        lo = val - hi - mid
        k_tail = jnp.where(lane == 0, hi, jnp.where(lane == 1, mid, jnp.where(lane == 2, lo, 0.0)))
        head = slice(h * FOX_HEAD, (h + 1) * FOX_HEAD)
        kh = k_ref[:, head].astype(F32)
        ss = jnp.dot((kh * kh).astype(BF16), ones, preferred_element_type=F32)
        kh = kh * lax.rsqrt(ss * (1.0 / FOX_HEAD) + EPS) * kn_ref[...]
        ka_ref[:, 2 * h * LANES:(2 * h + 1) * LANES] = kh.astype(BF16)
        ka_ref[:, (2 * h + 1) * LANES:(2 * h + 2) * LANES] = k_tail.astype(BF16)
        va_ref[:, 2 * h * LANES:(2 * h + 1) * LANES] = v_ref[:, head]
        va_ref[:, (2 * h + 1) * LANES:(2 * h + 2) * LANES] = v_tail


def _fox_kv_augment(f_logit, b_f, k_gain, kv, D):
    S = f_logit.shape[0]
    H = D // FOX_HEAD
    T = min(256, S)
    tri = jnp.asarray(np.tril(np.ones((T, T), np.float32)), BF16)
    aug = jax.ShapeDtypeStruct((S, 2 * D), BF16)
    return pl.pallas_call(
        functools.partial(_kvaug_body, T=T, H=H),
        grid=(S // T,),
        in_specs=[pl.BlockSpec((T, LANES), lambda i: (i, 0)),
                  pl.BlockSpec((1, LANES), lambda i: (0, 0)),
                  pl.BlockSpec((T, T), lambda i: (0, 0)),
                  pl.BlockSpec((1, FOX_HEAD), lambda i: (0, 0)),
                  pl.BlockSpec((T, D), lambda i: (i, 0)),
                  pl.BlockSpec((T, D), lambda i: (i, 1))],
        out_specs=[pl.BlockSpec((T, 2 * D), lambda i: (i, 0)), pl.BlockSpec((T, 2 * D), lambda i: (i, 0))],
        out_shape=[aug, aug],
        scratch_shapes=[pltpu.VMEM((1, LANES), F32)],
        compiler_params=_params("arbitrary"),
        name="fox_kv_augment",
    )(f_logit, b_f, tri, k_gain, kv, kv)


def _fox_body(q_ref, k_ref, v_ref, og_ref, o_ref, qa_ref, sa_ref, sb_ref, xa_ref, xb_ref, m_ref, acc_ref,
              *, tk, tr, G):
    i = pl.program_id(1)
    nsub = tk // tr
    bufs = ((sa_ref, xa_ref), (sb_ref, xb_ref))
    lane_q = lax.broadcasted_iota(jnp.int32, (G * tk, LANES), 1)
    qa_ref[:, :FOX_HEAD] = q_ref[...]
    qa_ref[:, FOX_HEAD:] = jnp.where(lane_q < 3, 1.0, 0.0).astype(BF16)
    m_ref[...] = jnp.full(m_ref.shape, NEG_BIG, F32)
    acc_ref[...] = jnp.zeros(acc_ref.shape, F32)

    def logits(g, cidx, buf):
        s_ref, x_ref = bufs[buf]
        kc = k_ref[pl.ds(pl.multiple_of(cidx * tk, tk), tk), :]
        for r in range(nsub):
            rows = slice(r * tr, (r + 1) * tr)
            s = lax.dot_general(qa_ref[g * tk + r * tr:g * tk + (r + 1) * tr, :], kc, (((1,), (1,)), ((), ())),
                                preferred_element_type=F32)
            s_ref[rows, :] = s
            x_ref[rows, :] = jnp.max(s, axis=-1, keepdims=True)

    def softmax_pv(g, cidx, buf, masked):
        s_ref, x_ref = bufs[buf]
        c0 = pl.multiple_of(cidx * tk, tk)
        for r in range(nsub):
            rows = slice(r * tr, (r + 1) * tr)
            grows = slice(g * tk + r * tr, g * tk + (r + 1) * tr)
            if masked:
                ncol = (r + 1) * tr
                row = lax.broadcasted_iota(jnp.int32, (tr, ncol), 0) + r * tr
                col = lax.broadcasted_iota(jnp.int32, (tr, ncol), 1)
                s = jnp.where(row >= col, s_ref[rows, :ncol], NEG_BIG)
                s_max = jnp.max(s, axis=-1, keepdims=True)
            else:
                ncol = tk
                s = s_ref[rows, :]
                s_max = x_ref[rows, :]
            vc = v_ref[pl.ds(c0, ncol), :]
            m_old = m_ref[grows, :]
            m_new = jnp.maximum(m_old, s_max)
            m_ref[grows, :] = m_new
            p = jnp.exp2(s - m_new).astype(BF16)
            acc_ref[grows, :] = (jnp.exp2(m_old - m_new) * acc_ref[grows, :]
                                 + jnp.dot(p, vc, preferred_element_type=F32))

    def full_blocks(cidx, carry):
        for g in range(G):
            if g + 1 < G:
                logits(g + 1, cidx, (g + 1) % 2)
            else:
                logits(0, cidx + 1, 0)
            softmax_pv(g, cidx, g % 2, False)
        return carry

    logits(0, 0, 0)
    lax.fori_loop(0, G * i, full_blocks, 0)
    tail = [(g, G * i + d, g == d) for d in range(G) for g in range(d, G)]
    for t, (g, cidx, masked) in enumerate(tail):
        if t + 1 < len(tail):
            logits(tail[t + 1][0], tail[t + 1][1], (t + 1) % 2)
        softmax_pv(g, cidx, t % 2, masked)

    for g in range(G):
        for r in range(nsub):
            rows = slice(g * tk + r * tr, g * tk + (r + 1) * tr)
            gate = jax.nn.sigmoid(og_ref[rows, :].astype(F32))
            inv_l = 1.0 / acc_ref[rows, FOX_HEAD:FOX_HEAD + 1]
            o_ref[rows, :] = (acc_ref[rows, :FOX_HEAD] * inv_l * gate).astype(o_ref.dtype)


def _fox_attention(qg, k_aug, v_aug, D):
    S = qg.shape[0]
    H = D // FOX_HEAD
    G = FOX_GROUPS
    tk = min(FOX_BLOCK, S // G)
    tq = G * tk
    return pl.pallas_call(
        functools.partial(_fox_body, tk=tk, tr=min(FOX_SUB, tk), G=G),
        grid=(H, S // tq),
        in_specs=[
            pl.BlockSpec((tq, FOX_HEAD), lambda h, i: (i, h)),
            pl.BlockSpec((S, 2 * LANES), lambda h, i: (0, h)),
            pl.BlockSpec((S, 2 * LANES), lambda h, i: (0, h)),
            pl.BlockSpec((tq, FOX_HEAD), lambda h, i: (i, H + h)),
        ],
        out_specs=pl.BlockSpec((tq, FOX_HEAD), lambda h, i: (i, h)),
        out_shape=jax.ShapeDtypeStruct((S, D), BF16),
        scratch_shapes=[pltpu.VMEM((tq, 2 * LANES), BF16), pltpu.VMEM((tk, tk), F32), pltpu.VMEM((tk, tk), F32),
                        pltpu.VMEM((tk, 1), F32), pltpu.VMEM((tk, 1), F32),
                        pltpu.VMEM((tq, 1), F32), pltpu.VMEM((tq, 2 * FOX_HEAD), F32)],
        compiler_params=_params("arbitrary", "arbitrary"),
        name="fox_attention",
    )(qg, k_aug, v_aug, qg)


def _pad_cols(w, n):
    return jnp.pad(w, ((0, 0), (0, n - w.shape[1])))


def _pad_rows(w, n):
    return jnp.pad(w, ((0, n - w.shape[0]), (0, 0)))


def _mlp(x, gain, shift, scale, gate, w_up, w_down, layer):
    h = _norm_mod(x, (gain, shift, scale))
    u = _matmul(h, _Weight(w_up, layer), out_dtype=BF16, epilogue=_ep_sqrelu, name=f"mlp_up_l{layer}")
    return _matmul_resid(u, _Weight(w_down, layer), x, gate, name=f"mlp_down_l{layer}")


def kernel(x, c, ada_w, ada_b, norm_mix, norm_mlp, w_mlp_up, w_mlp_down, gla_w_in, gla_w_alpha_up,
           gla_b_alpha, gla_o_norm, gla_w_out, kv_ada_w, kv_ada_b, kv_norm, kv_w, fox_b_f, fox_k_norm,
           fox_w_in, fox_q_norm, fox_w_out):
    B, S, D = x.shape
    assert B == 1 and ada_w.shape[0] == 2
    xs = x[0]
    c_col = c.reshape(D, 1)
    KW = (D // GLA_DV) * GLA_DK

    sh1, sc1, g1, sh2, sc2, g2 = jnp.split(_ada(c_col, ada_w, ada_b[0][None], 0), 6, axis=1)
    h = _norm_mod(xs, (norm_mix[0][None], sh1, sc1))
    n_main = 2 * KW + 2 * D
    w_in_t = jnp.swapaxes(gla_w_in, 1, 2)[0].astype(BF16)
    proj = _matmul(h, _Weight(w_in_t, transposed=True, n_cols=n_main), out_dtype=BF16, name="gla_in")
    a_lo = _matmul(h, _Weight(_pad_rows(w_in_t[n_main:], LANES), transposed=True), out_dtype=F32,
                   name="gla_in_lowrank")
    w_up = jnp.pad(gla_w_alpha_up[0], ((0, LANES - gla_w_alpha_up.shape[1]), (0, 0))).astype(BF16)
    y, (w_up_b, w_down_b, w_gout_b, w_fout_b) = _gla(
        proj, a_lo, w_up, gla_b_alpha[0][None], gla_o_norm[0][None], D,
        cast_srcs=(w_mlp_up.reshape(-1, w_mlp_up.shape[-1]), w_mlp_down.reshape(-1, w_mlp_down.shape[-1]),
                   gla_w_out[0], fox_w_out[0]))
    w_up_b = w_up_b.reshape(w_mlp_up.shape)
    w_down_b = w_down_b.reshape(w_mlp_down.shape)
    xs = _matmul_resid(y, _Weight(w_gout_b), xs, g1, name="gla_out")
    xs = _mlp(xs, norm_mlp[0][None], sh2, sc2, g2, w_up_b, w_down_b, 0)

    kshift, kscale = jnp.split(_ada(c_col, kv_ada_w, kv_ada_b[None], 0), 2, axis=1)
    sh1, sc1, g1, sh2, sc2, g2 = jnp.split(_ada(c_col, ada_w, ada_b[1][None], 1), 6, axis=1)
    hk, h = _norm_mod(xs, (kv_norm[None], kshift, kscale), (norm_mix[1][None], sh1, sc1))
    kv_w_t = kv_w.T.astype(BF16)
    kv = _matmul(hk, _Weight(kv_w_t, transposed=True, n_cols=2 * D), out_dtype=BF16, name="kv_proj")
    f_logit = _matmul(hk, _Weight(_pad_rows(kv_w_t[2 * D:], LANES), transposed=True), out_dtype=F32,
                      name="kv_forget")
    k_aug, v_aug = _fox_kv_augment(f_logit, _pad_cols(fox_b_f[None], LANES), fox_k_norm[None], kv, D)

    qg = _matmul_headnorm(h, _Weight(fox_w_in, 0), fox_q_norm[0][None], D, FOX_HEAD ** -0.5 * LOG2E,
                          name="fox_in")
    o = _fox_attention(qg, k_aug, v_aug, D)
    xs = _matmul_resid(o, _Weight(w_fout_b), xs, g1, name="fox_out")
    xs = _mlp(xs, norm_mlp[1][None], sh2, sc2, g2, w_up_b, w_down_b, 1)
    return xs[None]
```
